```python
import math
import jax
import jax.numpy as jnp
from jax import lax

D_MODEL = 2048
BATCH = 8
SEQ = 2048
DEPTH = 1

GRID_W = 64
CTX_LEN = 256
EPS = 1e-6
N_MOD = 6

ATTN_WIDTH = D_MODEL // 2
ATTN_HEAD_DIM = 64
ATTN_VDIM = 2 * ATTN_HEAD_DIM
ATTN_HEADS = ATTN_WIDTH // ATTN_VDIM
ATTN_QK = ATTN_HEADS * 2 * ATTN_HEAD_DIM
Q_BLOCK = 128
ROPE_BASE = 10000.0

REC_WIDTH = D_MODEL // 2
REC_KDIM = 128
REC_VDIM = 128
REC_HEADS = REC_WIDTH // REC_VDIM
REC_K = REC_HEADS * REC_KDIM
REC_CHUNK = 64

FFN_DIM = 256 * ((8 * D_MODEL // 3 + 255) // 256)
CONV_W = 3

IN_SPLITS = (ATTN_QK, ATTN_WIDTH, REC_K, REC_K, REC_WIDTH,
             ATTN_QK, REC_K, REC_WIDTH, D_MODEL, D_MODEL)
CTX_KV_WIDTH = ATTN_QK + ATTN_WIDTH + 2 * REC_K + REC_WIDTH
IN_WIDTH = CTX_KV_WIDTH + ATTN_QK + REC_K + REC_WIDTH + 2 * D_MODEL

kernel_name = "hybrid_diffattn_hgrn2_convffn_dit"


def rms_norm(x, w):
    xf = x.astype(jnp.float32)
    y = xf * lax.rsqrt(jnp.mean(xf * xf, axis=-1, keepdims=True) + EPS)
    return y.astype(x.dtype) * w


def modulate(x, w, shift, scale):
    return rms_norm(x, w) * (1.0 + scale) + shift


def split_cols(z, n):
    out = []
    start = 0
    for width in IN_SPLITS[:n]:
        out.append(z[..., start:start + width])
        start += width
    return out


def to_qk_heads(a):
    return a.reshape(a.shape[0], a.shape[1], ATTN_HEADS, 2, ATTN_HEAD_DIM)


def to_v_heads(a):
    return a.reshape(a.shape[0], a.shape[1], ATTN_HEADS, ATTN_VDIM)


def to_rec_heads(a):
    return a.reshape(a.shape[0], a.shape[1], REC_HEADS, -1)


def axial_rope_tables(n_tokens, dtype):
    rows = n_tokens // GRID_W
    r, col = jnp.meshgrid(jnp.arange(rows), jnp.arange(GRID_W), indexing='ij')
    pos = jnp.stack([r.reshape(-1), col.reshape(-1)], axis=-1).astype(jnp.float32)
    nq = ATTN_HEAD_DIM // 4
    inv = ROPE_BASE ** (-jnp.arange(nq, dtype=jnp.float32) / nq)
    ang = pos[:, :, None] * inv
    return jnp.cos(ang).astype(dtype), jnp.sin(ang).astype(dtype)


def apply_axial_rope(x, cos, sin):
    B, S, H, C, d = x.shape
    xr = x.reshape(B, S, H, C, 2, 2, d // 4)
    x1, x2 = xr[..., 0, :], xr[..., 1, :]
    cs = cos[None, :, None, None]
    sn = sin[None, :, None, None]
    out = jnp.stack([x1 * cs - x2 * sn, x2 * cs + x1 * sn], axis=-2)
    return out.reshape(B, S, H, C, d)


def diff_attn_core(q, k, v, lam):
    s = jnp.einsum('bqhcd,bkhcd->bhcqk', q, k).astype(jnp.float32) * (ATTN_HEAD_DIM ** -0.5)
    p = jax.nn.softmax(s, axis=-1)
    a = p[:, :, 0] - lam * p[:, :, 1]
    return jnp.einsum('bhqk,bkhe->bqhe', a.astype(v.dtype), v)


def diff_attn_latent(q, k, v, lam):
    B, S, H, C, d = q.shape
    qb = q.reshape(B, S // Q_BLOCK, Q_BLOCK, H, C, d).transpose(1, 0, 2, 3, 4, 5)
    ob = lax.map(lambda blk: diff_attn_core(blk, k, v, lam), qb)
    return ob.transpose(1, 0, 2, 3, 4).reshape(B, S, H, v.shape[-1])


def diff_attn_readout(o, subln_w, lam_init):
    B, T = o.shape[0], o.shape[1]
    return (rms_norm(o, subln_w) * (1.0 - lam_init)).reshape(B, T, ATTN_WIDTH)


def rec_gate(f_raw, lower):
    f = lower + (1.0 - lower) * jax.nn.sigmoid(f_raw.astype(jnp.float32))
    return to_rec_heads(1.0 - f), to_rec_heads(jnp.log(f))


def gla_scan(q, k, v, logf, s0):
    B, T, H, _ = k.shape
    dv = v.shape[-1]
    nc = T // REC_CHUNK

    def chunks(a):
        return a.astype(jnp.float32).reshape(B, nc, REC_CHUNK, H, a.shape[-1]).transpose(1, 0, 3, 2, 4)

    mask = jnp.tril(jnp.ones((REC_CHUNK, REC_CHUNK), dtype=bool))[None, None, :, :, None]
    with_out = q is not None
    xs = (chunks(k), chunks(v), chunks(logf)) + ((chunks(q),) if with_out else ())

    def step(state, inp):
        kc, vc, gc = inp[0], inp[1], inp[2]
        b = jnp.cumsum(gc, axis=2)
        b_end = b[:, :, -1:, :]
        new_state = (jnp.exp(b_end)[:, :, 0, :, None] * state
                     + jnp.einsum('bhsk,bhsv->bhkv', kc * jnp.exp(b_end - b), vc))
        if not with_out:
            return new_state, None
        qc = inp[3]
        rel = jnp.exp(jnp.where(mask, b[:, :, :, None, :] - b[:, :, None, :, :], -jnp.inf))
        scores = jnp.einsum('bhtk,bhtsk,bhsk->bhts', qc, rel, kc)
        out = (jnp.einsum('bhts,bhsv->bhtv', scores, vc)
               + jnp.einsum('bhtk,bhkv->bhtv', qc * jnp.exp(b), state))
        return new_state, out

    state, out = lax.scan(step, s0.astype(jnp.float32), xs)
    if with_out:
        out = out.transpose(1, 0, 3, 2, 4).reshape(B, T, H, dv).astype(v.dtype)
    return state, out


def rec_direction(lat, ctx_feats, reverse):
    if reverse:
        flip = lambda a: None if a is None else jnp.flip(a, axis=1)
    else:
        flip = lambda a: a
    qc, kc, vc, gc = [flip(a) for a in ctx_feats]
    s0 = jnp.zeros((kc.shape[0], REC_HEADS, REC_KDIM, REC_VDIM), jnp.float32)
    s_ctx, o_ctx = gla_scan(qc, kc, vc, gc, s0)
    q, k, v, g = [flip(a) for a in lat]
    _, o_lat = gla_scan(q, k, v, g, s_ctx)
    return flip(o_lat), flip(o_ctx)


def rec_readout(o, g, w):
    B, T = o.shape[0], o.shape[1]
    return rms_norm(o.reshape(B, T, REC_WIDTH), w) * jax.nn.silu(g)


def merge_branches(att, rec, gate_a, gate_r, w_branch_attn, w_branch_rec, w_out):
    y = jax.nn.sigmoid(gate_a) * (att @ w_branch_attn) + jax.nn.sigmoid(gate_r) * (rec @ w_branch_rec)
    return y @ w_out


def conv_ffn(h, w_up, conv_w, conv_b, w_down):
    u = h @ w_up
    T = u.shape[1]
    pad = CONV_W // 2
    up = jnp.pad(u, ((0, 0), (pad, pad), (0, 0)))
    u = conv_b + sum(up[:, j:j + T] * conv_w[j] for j in range(CONV_W))
    a, b = jnp.split(u, 2, axis=-1)
    return (jax.nn.silu(a) * b) @ w_down


def hybrid_layer(x, ctx, mod, mod_c, lam, lam_init, lb_f, lb_b, norm1_w, w_in, subln_w, rec_gnorm_w,
                 w_branch_attn, w_branch_rec, w_out, norm2_w, w_up, conv_w, conv_b, w_down, ctx_out):
    B, S, _ = x.shape
    sh1, sc1, g1, sh2, sc2, g2 = jnp.split(mod[:, None, :], N_MOD, axis=-1)
    csh1, csc1, cg1, csh2, csc2, cg2 = jnp.split(mod_c, N_MOD, axis=-1)

    h = modulate(x, norm1_w, sh1, sc1)
    hc = modulate(ctx, norm1_w, csh1, csc1)
    ak, av, rff, rfb, ri, aq, rq, rg, gate_a, gate_r = split_cols(h @ w_in, len(IN_SPLITS))
    if ctx_out:
        akc, avc, rffc, rfbc, ric, aqc, rqc, rgc, gate_ac, gate_rc = split_cols(hc @ w_in, len(IN_SPLITS))
    else:
        akc, avc, rffc, rfbc, ric = split_cols(hc @ w_in[:, :CTX_KV_WIDTH], 5)

    cos, sin = axial_rope_tables(S, x.dtype)
    k_all = jnp.concatenate([apply_axial_rope(to_qk_heads(ak), cos, sin), to_qk_heads(akc)], axis=1)
    v_all = jnp.concatenate([to_v_heads(av), to_v_heads(avc)], axis=1)
    o_att = diff_attn_latent(apply_axial_rope(to_qk_heads(aq), cos, sin), k_all, v_all, lam)
    att = diff_attn_readout(o_att, subln_w, lam_init)

    v_r, vc_r = to_rec_heads(ri), to_rec_heads(ric)
    q_r = to_rec_heads(jax.nn.silu(rq))
    qc_r = to_rec_heads(jax.nn.silu(rqc)) if ctx_out else None
    kf, logf_f = rec_gate(rff, lb_f)
    kb, logf_b = rec_gate(rfb, lb_b)
    kfc, logfc_f = rec_gate(rffc, lb_f)
    kbc, logfc_b = rec_gate(rfbc, lb_b)
    o_f, oc_f = rec_direction((q_r, kf, v_r, logf_f), (qc_r, kfc, vc_r, logfc_f), False)
    o_b, oc_b = rec_direction((q_r, kb, v_r, logf_b), (qc_r, kbc, vc_r, logfc_b), True)
    rec = rec_readout(o_f + o_b, rg, rec_gnorm_w)

    x = x + g1 * merge_branches(att, rec, gate_a, gate_r, w_branch_attn, w_branch_rec, w_out)
    x = x + g2 * conv_ffn(modulate(x, norm2_w, sh2, sc2), w_up, conv_w, conv_b, w_down)

    if ctx_out:
        oc_att = diff_attn_core(to_qk_heads(aqc), to_qk_heads(akc), to_v_heads(avc), lam)
        att_c = diff_attn_readout(oc_att, subln_w, lam_init)
        rec_c = rec_readout(oc_f + oc_b, rgc, rec_gnorm_w)
        ctx = ctx + cg1 * merge_branches(att_c, rec_c, gate_ac, gate_rc, w_branch_attn, w_branch_rec, w_out)
        ctx = ctx + cg2 * conv_ffn(modulate(ctx, norm2_w, csh2, csc2), w_up, conv_w, conv_b, w_down)
    return x, ctx


def setup_inputs(seed: int = 0) -> dict:
    key = jax.random.key(seed)
    ks = jax.random.split(key, 24)
    f32 = jnp.float32

    def nrm(k, shape, fan_in):
        return jax.random.normal(k, shape, f32) * fan_in ** -0.5

    def gain(k, shape):
        return 1.0 + 0.02 * jax.random.normal(k, shape, f32)

    def small(k, shape, s):
        return s * jax.random.normal(k, shape, f32)

    return {
        "x": jax.random.normal(ks[0], (BATCH, SEQ, D_MODEL), f32),
        "c": jax.random.normal(ks[1], (BATCH, D_MODEL), f32),
        "ctx": jax.random.normal(ks[2], (BATCH, CTX_LEN, D_MODEL), f32),
        "c_ctx": jax.random.normal(ks[3], (D_MODEL,), f32),
        "w_mod": nrm(ks[4], (DEPTH, D_MODEL, N_MOD * D_MODEL), D_MODEL),
        "b_mod": small(ks[5], (DEPTH, N_MOD * D_MODEL), 0.02),
        "norm1_w": gain(ks[6], (DEPTH, D_MODEL)),
        "w_in": nrm(ks[7], (DEPTH, D_MODEL, IN_WIDTH), D_MODEL),
        "lam_q1": small(ks[8], (DEPTH, ATTN_HEAD_DIM), 0.1),
        "lam_k1": small(ks[9], (DEPTH, ATTN_HEAD_DIM), 0.1),
        "lam_q2": small(ks[10], (DEPTH, ATTN_HEAD_DIM), 0.1),
        "lam_k2": small(ks[11], (DEPTH, ATTN_HEAD_DIM), 0.1),
        "subln_w": gain(ks[12], (DEPTH, ATTN_VDIM)),
        "rec_lb": small(ks[13], (2, DEPTH + 1, REC_K), 0.5),
        "rec_gnorm_w": gain(ks[14], (DEPTH, REC_WIDTH)),
        "w_branch_attn": nrm(ks[15], (DEPTH, ATTN_WIDTH, D_MODEL), ATTN_WIDTH),
        "w_branch_rec": nrm(ks[16], (DEPTH, REC_WIDTH, D_MODEL), REC_WIDTH),
        "w_out": nrm(ks[17], (DEPTH, D_MODEL, D_MODEL), D_MODEL),
        "norm2_w": gain(ks[18], (DEPTH, D_MODEL)),
        "w_up": nrm(ks[19], (DEPTH, D_MODEL, 2 * FFN_DIM), D_MODEL),
        "conv_w": nrm(ks[20], (DEPTH, CONV_W, 2 * FFN_DIM), CONV_W),
        "conv_b": small(ks[21], (DEPTH, 2 * FFN_DIM), 0.02),
        "w_down": nrm(ks[22], (DEPTH, FFN_DIM, D_MODEL), FFN_DIM),
        "final_norm_w": gain(ks[23], (D_MODEL,)),
    }


def reference(x, c, ctx, c_ctx, w_mod, b_mod, norm1_w, w_in, lam_q1, lam_k1, lam_q2, lam_k2, subln_w,
              rec_lb, rec_gnorm_w, w_branch_attn, w_branch_rec, w_out, norm2_w, w_up, conv_w, conv_b,
              w_down, final_norm_w):
    lower = jnp.cumsum(jax.nn.softmax(rec_lb.astype(jnp.float32), axis=1), axis=1)
    for l in range(DEPTH):
        mod = jax.nn.silu(c) @ w_mod[l] + b_mod[l]
        mod_c = jax.nn.silu(c_ctx) @ w_mod[l] + b_mod[l]
        lam_init = 0.8 - 0.6 * math.exp(-0.3 * l)
        lam = (jnp.exp(jnp.sum(lam_q1[l].astype(jnp.float32) * lam_k1[l].astype(jnp.float32)))
               - jnp.exp(jnp.sum(lam_q2[l].astype(jnp.float32) * lam_k2[l].astype(jnp.float32)))
               + lam_init)
        x, ctx = hybrid_layer(x, ctx, mod, mod_c, lam, lam_init, lower[0, l], lower[1, l],
                              norm1_w[l], w_in[l], subln_w[l], rec_gnorm_w[l],
                              w_branch_attn[l], w_branch_rec[l], w_out[l], norm2_w[l],
                              w_up[l], conv_w[l], conv_b[l], w_down[l], l < DEPTH - 1)
    return rms_norm(x, final_norm_w)
```

```python
import functools
import math

import jax
import jax.numpy as jnp
from jax import lax
from jax.experimental import pallas as pl
from jax.experimental.pallas import tpu as pltpu

F32 = jnp.float32
BF16 = jnp.bfloat16

D_MODEL = 2048
BATCH = 8
SEQ = 2048
GRID_W = 64
CTX_LEN = 256
EPS = 1e-6
N_MOD = 6
HEADS = 8
HEAD_DIM = 64
HEAD_W = 128
ATTN_WIDTH = 1024
REC_WIDTH = 1024
FFN_DIM = 5632
IN_WIDTH = 12288
CTX_KV_WIDTH = 5120
ROPE_BASE = 10000.0
LAM_INIT = 0.8 - 0.6 * math.exp(-0.3 * 0)

COL_AK, COL_AV, COL_RFF, COL_RFB, COL_RI, COL_AQ, COL_RQ = 0, 8, 16, 24, 32, 40, 48
MOD_ROWS = 16
CTX_MOD_ROW = BATCH

CHUNK = 64
SUB = 16
VMEM_LIMIT = 56 * 1024 * 1024


def _cparams(n_axes, vmem=None):
    return pltpu.CompilerParams(
        dimension_semantics=("arbitrary",) * n_axes,
        vmem_limit_bytes=vmem,
    )


def _sigmoid(x):
    return 1.0 / (1.0 + jnp.exp(-x))


def _rms(x, w):
    return x * lax.rsqrt(jnp.mean(x * x, axis=-1, keepdims=True) + EPS) * w


def _mod_kernel(c_ref, w_ref, b_ref, o_ref):
    c = c_ref[...]
    a = (c * _sigmoid(c)).astype(BF16)
    o_ref[...] = jnp.dot(a, w_ref[...].astype(BF16), preferred_element_type=F32) + b_ref[...]


def _mod_call(c_all, w_mod, b_mod):
    tn = 1024
    n = w_mod.shape[1]
    return pl.pallas_call(
        _mod_kernel,
        grid=(n // tn,),
        in_specs=[
            pl.BlockSpec((MOD_ROWS, D_MODEL), lambda j: (0, 0)),
            pl.BlockSpec((D_MODEL, tn), lambda j: (0, j)),
            pl.BlockSpec((1, tn), lambda j: (0, j)),
        ],
        out_specs=pl.BlockSpec((MOD_ROWS, tn), lambda j: (0, j)),
        out_shape=jax.ShapeDtypeStruct((MOD_ROWS, n), F32),
        compiler_params=_cparams(1, VMEM_LIMIT),
        name="mod",
    )(c_all, w_mod, b_mod)


def _inproj_kernel(x_ref, nw_ref, sh_ref, sc_ref, w_ref, o_ref, h_ref):
    @pl.when(pl.program_id(1) == 0)
    def _():
        h = _rms(x_ref[...], nw_ref[...]) * (1.0 + sc_ref[0]) + sh_ref[0]
        h_ref[...] = h.astype(BF16)

    o_ref[...] = jnp.dot(h_ref[...], w_ref[...], preferred_element_type=F32)


def _inproj_call(x2d, norm_w, mod3, w_bf16, n_out, mod_row_of_tile, tm, name):
    m = x2d.shape[0]
    tn = 512
    return pl.pallas_call(
        _inproj_kernel,
        grid=(m // tm, n_out // tn),
        in_specs=[
            pl.BlockSpec((tm, D_MODEL), lambda i, j: (i, 0)),
            pl.BlockSpec((1, D_MODEL), lambda i, j: (0, 0)),
            pl.BlockSpec((1, 1, D_MODEL), lambda i, j: (mod_row_of_tile(i), 0, 0)),
            pl.BlockSpec((1, 1, D_MODEL), lambda i, j: (mod_row_of_tile(i), 0, 1)),
            pl.BlockSpec((D_MODEL, tn), lambda i, j: (0, j)),
        ],
        out_specs=pl.BlockSpec((tm, tn), lambda i, j: (i, j)),
        out_shape=jax.ShapeDtypeStruct((m, n_out), F32),
        scratch_shapes=[pltpu.VMEM((tm, D_MODEL), BF16)],
        compiler_params=_cparams(2, VMEM_LIMIT),
        name=name,
    )(x2d, norm_w, mod3, mod3, w_bf16)


def _rope(x, c, sa, sb):
    return x * c + pltpu.roll(x, HEAD_W - 16, 1) * sa + pltpu.roll(x, 16, 1) * sb


def _attn_kernel(lamp_ref, q_ref, k_ref, v_ref, kc_ref, vc_ref, cq_ref, saq_ref, sbq_ref,
                 ck_ref, sak_ref, sbk_ref, subln_ref, o_ref, k_s, v_s):
    tq = q_ref.shape[1]

    @pl.when(pl.program_id(2) == 0)
    def _():
        k = _rope(k_ref[0], ck_ref[...], sak_ref[...], sbk_ref[...])
        k_s[0:SEQ, :] = k.astype(BF16)
        k_s[SEQ:SEQ + CTX_LEN, :] = kc_ref[0].astype(BF16)
        v_s[0:SEQ, :] = v_ref[0].astype(BF16)
        v_s[SEQ:SEQ + CTX_LEN, :] = vc_ref[0].astype(BF16)

    lp = lamp_ref[...]
    lam = (jnp.exp(jnp.sum(lp[0:1] * lp[1:2], axis=-1, keepdims=True))
           - jnp.exp(jnp.sum(lp[2:3] * lp[3:4], axis=-1, keepdims=True)) + LAM_INIT)

    q = _rope(q_ref[0], cq_ref[...], saq_ref[...], sbq_ref[...]) * (HEAD_DIM ** -0.5)
    lane = lax.broadcasted_iota(jnp.int32, q.shape, 1)
    q0 = jnp.where(lane < HEAD_DIM, q, 0.0).astype(BF16)
    q1 = jnp.where(lane >= HEAD_DIM, q, 0.0).astype(BF16)
    qq = jnp.concatenate([q0, q1], axis=0)
    s = lax.dot_general(qq, k_s[...], (((1,), (1,)), ((), ())), preferred_element_type=F32)
    m = jnp.max(s, axis=-1, keepdims=True)
    e = jnp.exp(s - m)
    p = e / jnp.sum(e, axis=-1, keepdims=True)
    a = p[0:tq] - lam * p[tq:2 * tq]
    o = jnp.dot(a.astype(BF16), v_s[...], preferred_element_type=F32)
    o = _rms(o, subln_ref[...]) * (1.0 - LAM_INIT)
    o_ref[0] = o.astype(BF16)


def _attn_call(z3, zc3, lam_params, rope_tabs, subln_w):
    tq = 256
    c_t, sa_t, sb_t = rope_tabs
    qmap = lambda b, h, i: (i, 0)
    full = lambda b, h, i: (0, 0)
    return pl.pallas_call(
        _attn_kernel,
        grid=(BATCH, HEADS, SEQ // tq),
        in_specs=[
            pl.BlockSpec((4, HEAD_DIM), full),
            pl.BlockSpec((1, tq, HEAD_W), lambda b, h, i: (b, i, COL_AQ + h)),
            pl.BlockSpec((1, SEQ, HEAD_W), lambda b, h, i: (b, 0, COL_AK + h)),
            pl.BlockSpec((1, SEQ, HEAD_W), lambda b, h, i: (b, 0, COL_AV + h)),
            pl.BlockSpec((1, CTX_LEN, HEAD_W), lambda b, h, i: (b, 0, COL_AK + h)),
            pl.BlockSpec((1, CTX_LEN, HEAD_W), lambda b, h, i: (b, 0, COL_AV + h)),
            pl.BlockSpec((tq, HEAD_W), qmap),
            pl.BlockSpec((tq, HEAD_W), qmap),
            pl.BlockSpec((tq, HEAD_W), qmap),
            pl.BlockSpec((SEQ, HEAD_W), full),
            pl.BlockSpec((SEQ, HEAD_W), full),
            pl.BlockSpec((SEQ, HEAD_W), full),
            pl.BlockSpec((1, HEAD_W), full),
        ],
        out_specs=pl.BlockSpec((1, tq, HEAD_W), lambda b, h, i: (b, i, h)),
        out_shape=jax.ShapeDtypeStruct((BATCH, SEQ, ATTN_WIDTH), BF16),
        scratch_shapes=[pltpu.VMEM((SEQ + CTX_LEN, HEAD_W), BF16),
                        pltpu.VMEM((SEQ + CTX_LEN, HEAD_W), BF16)],
        compiler_params=_cparams(3, VMEM_LIMIT),
        name="attn",
    )(lam_params, z3, z3, z3, zc3, zc3, c_t, sa_t, sb_t, c_t, sa_t, sb_t, subln_w)


def _tri(rev):
    r = lax.broadcasted_iota(jnp.int32, (CHUNK, CHUNK), 0)
    c = lax.broadcasted_iota(jnp.int32, (CHUNK, CHUNK), 1)
    return jnp.where((c >= r) if rev else (c <= r), 1.0, 0.0).astype(F32)


def _gate(f_raw, lower):
    f = lower + (1.0 - lower) * _sigmoid(f_raw)
    return 1.0 - f, jnp.log(f)


def _decay_sums(logf, rev):
    b = jnp.dot(_tri(rev), logf, precision=lax.Precision.HIGHEST, preferred_element_type=F32)
    b_tot = b[0:1] if rev else b[CHUNK - 1:CHUNK]
    return b, b_tot


def _state_update(st, k, v, b, b_tot):
    kg = (k * jnp.exp(b_tot - b)).astype(BF16)
    upd = lax.dot_general(v.astype(BF16), kg, (((0,), (0,)), ((), ())), preferred_element_type=F32)
    return jnp.exp(b_tot) * st + upd


def _diag_block(qb, kb, bb, vb, rev):
    rows = lax.broadcasted_iota(jnp.int32, (8, HEAD_W), 0)
    acc = [jnp.zeros((8, HEAD_W), F32) for _ in range(SUB // 8)]
    for s in range(SUB):
        bs = bb[s:s + 1]
        ks = kb[s:s + 1]
        vs = vb[s:s + 1]
        for u in range(SUB // 8):
            t_lo, t_hi = 8 * u, 8 * u + 7
            if (t_lo > s) if rev else (t_hi < s):
                continue
            w = jnp.exp(bb[t_lo:t_lo + 8] - bs)
            if (t_hi > s) if rev else (t_lo < s):
                keep = (rows + t_lo <= s) if rev else (rows + t_lo >= s)
                w = jnp.where(keep, w, 0.0)
            col = jnp.sum(qb[t_lo:t_lo + 8] * w * ks, axis=-1, keepdims=True)
            acc[u] = acc[u] + col * vs
    return jnp.concatenate(acc, axis=0)


def _chunk_out(st, q, k, v, b, rev):
    o = lax.dot_general((q * jnp.exp(b)).astype(BF16), st.astype(BF16),
                        (((1,), (1,)), ((), ())), preferred_element_type=F32)
    v16 = v.astype(BF16)
    blocks = []
    nsub = CHUNK // SUB
    for i in range(nsub):
        lo, hi = SUB * i, SUB * (i + 1)
        qb, bb = q[lo:hi], b[lo:hi]
        ob = _diag_block(qb, k[lo:hi], bb, v[lo:hi], rev)
        if rev:
            e_lo, e_hi = hi, CHUNK
            ref_row = b[hi:hi + 1] if hi < CHUNK else None
        else:
            e_lo, e_hi = 0, lo
            ref_row = b[lo - 1:lo] if lo > 0 else None
        if ref_row is not None:
            qt = (qb * jnp.exp(bb - ref_row)).astype(BF16)
            kt = (k[e_lo:e_hi] * jnp.exp(ref_row - b[e_lo:e_hi])).astype(BF16)
            a = lax.dot_general(qt, kt, (((1,), (1,)), ((), ())), preferred_element_type=F32)
            ob = ob + jnp.dot(a.astype(BF16), v16[e_lo:e_hi], preferred_element_type=F32)
        blocks.append(ob)
    return o + jnp.concatenate(blocks, axis=0)


def _hgrn2_kernel(lb_ref, ff_ref, fb_ref, v_ref, q_ref, ffc_ref, fbc_ref, vc_ref, o_ref):
    lb = lb_ref[...]
    n_lat = SEQ // CHUNK
    n_ctx = CTX_LEN // CHUNK

    for rev in (False, True):
        d = 1 if rev else 0
        l0, l1 = lb[d, 0:1], lb[d, 1:2]
        mx = jnp.maximum(l0, l1)
        e0, e1 = jnp.exp(l0 - mx), jnp.exp(l1 - mx)
        lower = e0 / (e0 + e1)
        f_lat = fb_ref if rev else ff_ref
        f_ctx = fbc_ref if rev else ffc_ref

        def ctx_body(c, st, rev=rev, lower=lower, f_ctx=f_ctx):
            off = pl.multiple_of(((n_ctx - 1 - c) if rev else c) * CHUNK, CHUNK)
            k, logf = _gate(f_ctx[0, pl.ds(off, CHUNK), :], lower)
            b, b_tot = _decay_sums(logf, rev)
            return _state_update(st, k, vc_ref[0, pl.ds(off, CHUNK), :], b, b_tot)

        def lat_body(c, st, rev=rev, lower=lower, f_lat=f_lat):
            off = pl.multiple_of(((n_lat - 1 - c) if rev else c) * CHUNK, CHUNK)
            k, logf = _gate(f_lat[0, pl.ds(off, CHUNK), :], lower)
            b, b_tot = _decay_sums(logf, rev)
            qr = q_ref[0, pl.ds(off, CHUNK), :]
            q = qr * _sigmoid(qr)
            v = v_ref[0, pl.ds(off, CHUNK), :]
            st_new = _state_update(st, k, v, b, b_tot)
            o = _chunk_out(st, q, k, v, b, rev)
            if rev:
                o_ref[0, pl.ds(off, CHUNK), :] = o_ref[0, pl.ds(off, CHUNK), :] + o
            else:
                o_ref[0, pl.ds(off, CHUNK), :] = o
            return st_new

        st = jnp.zeros((HEAD_W, HEAD_W), F32)
        st = lax.fori_loop(0, n_ctx, ctx_body, st)
        lax.fori_loop(0, n_lat, lat_body, st)


def _hgrn2_call(z3, zc3, rec_lb):
    lat = lambda col: pl.BlockSpec((1, SEQ, HEAD_W), lambda b, h: (b, 0, col + h))
    ctx = lambda col: pl.BlockSpec((1, CTX_LEN, HEAD_W), lambda b, h: (b, 0, col + h))
    return pl.pallas_call(
        _hgrn2_kernel,
        grid=(BATCH, HEADS),
        in_specs=[
            pl.BlockSpec((2, 2, HEAD_W), lambda b, h: (0, 0, h)),
            lat(COL_RFF), lat(COL_RFB), lat(COL_RI), lat(COL_RQ),
            ctx(COL_RFF), ctx(COL_RFB), ctx(COL_RI),
        ],
        out_specs=pl.BlockSpec((1, SEQ, HEAD_W), lambda b, h: (b, 0, h)),
        out_shape=jax.ShapeDtypeStruct((BATCH, SEQ, REC_WIDTH), F32),
        compiler_params=_cparams(2, VMEM_LIMIT),
        name="hgrn2",
    )(rec_lb, z3, z3, z3, z3, zc3, zc3, zc3)


def _merge_kernel(att_ref, orec_ref, rg_ref, ga_ref, gr_ref, x_ref, g1_ref, sh2_ref, sc2_ref,
                  gnw_ref, n2w_ref, wba_ref, wbr_ref, wout_ref, x1_ref, h2_ref):
    rg = rg_ref[...]
    rec = _rms(orec_ref[...], gnw_ref[...]) * (rg * _sigmoid(rg))
    ya = jnp.dot(att_ref[...], wba_ref[...], preferred_element_type=F32)
    yr = jnp.dot(rec.astype(BF16), wbr_ref[...], preferred_element_type=F32)
    y = _sigmoid(ga_ref[...]) * ya + _sigmoid(gr_ref[...]) * yr
    x1 = x_ref[...] + g1_ref[0] * jnp.dot(y.astype(BF16), wout_ref[...], preferred_element_type=F32)
    x1_ref[...] = x1
    h2 = _rms(x1, n2w_ref[...]) * (1.0 + sc2_ref[0]) + sh2_ref[0]
    h2_ref[...] = h2.astype(BF16)


def _merge_call(att2d, orec2d, z2d, x2d, mod3, gnorm_w, norm2_w, wba, wbr, wout):
    tm = 256
    m = x2d.shape[0]
    row = lambda i: (i, 0)
    const = lambda i: (0, 0)
    modspec = lambda k: pl.BlockSpec((1, 1, D_MODEL), lambda i: ((i * tm) // SEQ, 0, k))
    resident = lambda shape: pl.BlockSpec(shape, const, pipeline_mode=pl.Buffered(1))
    return pl.pallas_call(
        _merge_kernel,
        grid=(m // tm,),
        in_specs=[
            pl.BlockSpec((tm, ATTN_WIDTH), row),
            pl.BlockSpec((tm, REC_WIDTH), row),
            pl.BlockSpec((tm, REC_WIDTH), lambda i: (i, 7)),
            pl.BlockSpec((tm, D_MODEL), lambda i: (i, 4)),
            pl.BlockSpec((tm, D_MODEL), lambda i: (i, 5)),
            pl.BlockSpec((tm, D_MODEL), row),
            modspec(2), modspec(3), modspec(4),
            pl.BlockSpec((1, REC_WIDTH), const),
            pl.BlockSpec((1, D_MODEL), const),
            resident((ATTN_WIDTH, D_MODEL)),
            resident((REC_WIDTH, D_MODEL)),
            resident((D_MODEL, D_MODEL)),
        ],
        out_specs=[pl.BlockSpec((tm, D_MODEL), row), pl.BlockSpec((tm, D_MODEL), row)],
        out_shape=[jax.ShapeDtypeStruct((m, D_MODEL), F32), jax.ShapeDtypeStruct((m, D_MODEL), BF16)],
        compiler_params=_cparams(1, VMEM_LIMIT),
        name="merge",
    )(att2d, orec2d, z2d, z2d, z2d, x2d, mod3, mod3, mod3, gnorm_w, norm2_w, wba, wbr, wout)


def _seq_conv(u, cw, cb):
    n = u.shape[0]
    rows = lax.broadcasted_iota(jnp.int32, u.shape, 0)
    prev = jnp.where(rows == 0, 0.0, pltpu.roll(u, 1, 0))
    nxt = jnp.where(rows == n - 1, 0.0, pltpu.roll(u, n - 1, 0))
    return cb + (prev * cw[0:1] + u * cw[1:2] + nxt * cw[2:3])


def _ffn_up_kernel(h_ref, wa_ref, wb_ref, cwa_ref, cwb_ref, cba_ref, cbb_ref, g_ref):
    h = h_ref[0]
    ua = jnp.dot(h, wa_ref[...], preferred_element_type=F32)
    ub = jnp.dot(h, wb_ref[...], preferred_element_type=F32)
    a = _seq_conv(ua, cwa_ref[...], cba_ref[...])
    b = _seq_conv(ub, cwb_ref[...], cbb_ref[...])
    g_ref[0] = (a * _sigmoid(a) * b).astype(BF16)


def _ffn_up_call(h3, w_up, conv_w, conv_b):
    tn = 256
    nb = FFN_DIM // tn
    return pl.pallas_call(
        _ffn_up_kernel,
        grid=(BATCH, nb),
        in_specs=[
            pl.BlockSpec((1, SEQ, D_MODEL), lambda b, j: (b, 0, 0)),
            pl.BlockSpec((D_MODEL, tn), lambda b, j: (0, j)),
            pl.BlockSpec((D_MODEL, tn), lambda b, j: (0, nb + j)),
            pl.BlockSpec((3, tn), lambda b, j: (0, j)),
            pl.BlockSpec((3, tn), lambda b, j: (0, nb + j)),
            pl.BlockSpec((1, tn), lambda b, j: (0, j)),
            pl.BlockSpec((1, tn), lambda b, j: (0, nb + j)),
        ],
        out_specs=pl.BlockSpec((1, SEQ, tn), lambda b, j: (b, 0, j)),
        out_shape=jax.ShapeDtypeStruct((BATCH, SEQ, FFN_DIM), BF16),
        compiler_params=_cparams(2, VMEM_LIMIT),
        name="ffn_up",
    )(h3, w_up, w_up, conv_w, conv_w, conv_b, conv_b)


def _ffn_down_kernel(g_ref, w_ref, x1_ref, g2_ref, fw_ref, o_ref):
    y = jnp.dot(g_ref[...], w_ref[...], preferred_element_type=F32)
    x2 = x1_ref[...] + g2_ref[0] * y
    o_ref[...] = _rms(x2, fw_ref[...])


def _ffn_down_call(g2d, w_down, x1, mod3, final_w):
    tm = 256
    m = g2d.shape[0]
    return pl.pallas_call(
        _ffn_down_kernel,
        grid=(m // tm,),
        in_specs=[
            pl.BlockSpec((tm, FFN_DIM), lambda i: (i, 0)),
            pl.BlockSpec((FFN_DIM, D_MODEL), lambda i: (0, 0), pipeline_mode=pl.Buffered(1)),
            pl.BlockSpec((tm, D_MODEL), lambda i: (i, 0)),
            pl.BlockSpec((1, 1, D_MODEL), lambda i: ((i * tm) // SEQ, 0, 5)),
            pl.BlockSpec((1, D_MODEL), lambda i: (0, 0)),
        ],
        out_specs=pl.BlockSpec((tm, D_MODEL), lambda i: (i, 0)),
        out_shape=jax.ShapeDtypeStruct((m, D_MODEL), F32),
        compiler_params=_cparams(1, VMEM_LIMIT),
        name="ffn_down",
    )(g2d, w_down, x1, mod3, final_w)


def _rope_tables():
    rows = SEQ // GRID_W
    r, col = jnp.meshgrid(jnp.arange(rows), jnp.arange(GRID_W), indexing="ij")
    pos = jnp.stack([r.reshape(-1), col.reshape(-1)], axis=-1).astype(F32)
    nq = HEAD_DIM // 4
    inv = ROPE_BASE ** (-jnp.arange(nq, dtype=F32) / nq)
    ang = pos[:, :, None] * inv
    cos, sin = jnp.cos(ang), jnp.sin(ang)
    lane = jnp.arange(HEAD_W)
    axis = (lane % HEAD_DIM) // (2 * nq)
    second = ((lane % (2 * nq)) // nq) == 1
    freq = lane % nq
    c_t = cos[:, axis, freq]
    s_t = sin[:, axis, freq]
    sa_t = jnp.where(second[None, :], 0.0, -s_t)
    sb_t = jnp.where(second[None, :], s_t, 0.0)
    return c_t, sa_t, sb_t


def kernel(x, c, ctx, c_ctx, w_mod, b_mod, norm1_w, w_in, lam_q1, lam_k1, lam_q2, lam_k2, subln_w,
           rec_lb, rec_gnorm_w, w_branch_attn, w_branch_rec, w_out, norm2_w, w_up, conv_w, conv_b,
           w_down, final_norm_w):
    m_lat = BATCH * SEQ
    c_all = jnp.concatenate([c, c_ctx[None, :], jnp.zeros((MOD_ROWS - BATCH - 1, D_MODEL), F32)], axis=0)
    mod = _mod_call(c_all, w_mod[0], b_mod[0][None, :])
    mod3 = mod.reshape(MOD_ROWS, 1, N_MOD * D_MODEL)

    w_in16 = w_in[0].astype(BF16)
    n1w = norm1_w[0][None, :]
    x2d = x.reshape(m_lat, D_MODEL)
    tm_lat = 1024
    z = _inproj_call(x2d, n1w, mod3, w_in16, IN_WIDTH, lambda i: (i * tm_lat) // SEQ, tm_lat, "inproj")
    zc = _inproj_call(ctx.reshape(BATCH * CTX_LEN, D_MODEL), n1w, mod3, w_in16, CTX_KV_WIDTH,
                      lambda i: CTX_MOD_ROW, 1024, "inproj_ctx")
    z3 = z.reshape(BATCH, SEQ, IN_WIDTH)
    zc3 = zc.reshape(BATCH, CTX_LEN, CTX_KV_WIDTH)

    lam_params = jnp.stack([lam_q1[0], lam_k1[0], lam_q2[0], lam_k2[0]], axis=0)
    att = _attn_call(z3, zc3, lam_params, _rope_tables(), subln_w[0][None, :])
    orec = _hgrn2_call(z3, zc3, rec_lb)

    x1, h2 = _merge_call(att.reshape(m_lat, ATTN_WIDTH), orec.reshape(m_lat, REC_WIDTH), z, x2d, mod3,
                         rec_gnorm_w[0][None, :], norm2_w[0][None, :],
                         w_branch_attn[0].astype(BF16), w_branch_rec[0].astype(BF16), w_out[0].astype(BF16))
    g = _ffn_up_call(h2.reshape(BATCH, SEQ, D_MODEL), w_up[0].astype(BF16), conv_w[0], conv_b[0][None, :])
    out = _ffn_down_call(g.reshape(m_lat, FFN_DIM), w_down[0].astype(BF16), x1, mod3, final_norm_w[None, :])
    return out.reshape(BATCH, SEQ, D_MODEL)
```

```python
import math

import jax
import jax.numpy as jnp
from jax import lax
from jax.experimental import pallas as pl
from jax.experimental.pallas import tpu as pltpu

F32 = jnp.float32
BF16 = jnp.bfloat16

D_MODEL = 2048
BATCH = 8
SEQ = 2048
GRID_W = 64
CTX_LEN = 256
EPS = 1e-6
N_MOD = 6
HEADS = 8
HEAD_DIM = 64
HEAD_W = 128
ATTN_WIDTH = 1024
REC_WIDTH = 1024
FFN_DIM = 5632
IN_WIDTH = 12288
CTX_KV_WIDTH = 5120
ROPE_BASE = 10000.0
LAM_INIT = 0.8 - 0.6 * math.exp(-0.3 * 0)

COL_AK, COL_AV, COL_RFF, COL_RFB, COL_RI, COL_AQ, COL_RQ = 0, 8, 16, 24, 32, 40, 48
MOD_ROWS = 16
CTX_MOD_ROW = BATCH

CHUNK = 64
SUB = 16
VMEM_LIMIT = 56 * 1024 * 1024


def _cparams(n_axes, vmem=None):
    return pltpu.CompilerParams(
        dimension_semantics=("arbitrary",) * n_axes,
        vmem_limit_bytes=vmem,
    )


def _sigmoid(x):
    return 1.0 / (1.0 + jnp.exp(-x))


def _rms(x, w):
    return x * lax.rsqrt(jnp.mean(x * x, axis=-1, keepdims=True) + EPS) * w


def _mod_kernel(c_ref, w_ref, b_ref, o_ref):
    c = c_ref[...]
    a = (c * _sigmoid(c)).astype(BF16)
    o_ref[...] = jnp.dot(a, w_ref[...].astype(BF16), preferred_element_type=F32) + b_ref[...]


def _mod_call(c_all, w_mod, b_mod):
    tn = 1024
    n = w_mod.shape[1]
    return pl.pallas_call(
        _mod_kernel,
        grid=(n // tn,),
        in_specs=[
            pl.BlockSpec((MOD_ROWS, D_MODEL), lambda j: (0, 0)),
            pl.BlockSpec((D_MODEL, tn), lambda j: (0, j)),
            pl.BlockSpec((1, tn), lambda j: (0, j)),
        ],
        out_specs=pl.BlockSpec((MOD_ROWS, tn), lambda j: (0, j)),
        out_shape=jax.ShapeDtypeStruct((MOD_ROWS, n), F32),
        compiler_params=_cparams(1, VMEM_LIMIT),
        name="mod",
    )(c_all, w_mod, b_mod)


def _inproj_kernel(x_ref, nw_ref, sh_ref, sc_ref, w_ref, o_ref, h_ref):
    @pl.when(pl.program_id(1) == 0)
    def _():
        h = _rms(x_ref[...], nw_ref[...]) * (1.0 + sc_ref[0]) + sh_ref[0]
        h_ref[...] = h.astype(BF16)

    o_ref[...] = jnp.dot(h_ref[...], w_ref[...], preferred_element_type=F32)


def _inproj_call(x2d, norm_w, mod3, w_bf16, n_out, mod_row_of_tile, tm, name):
    m = x2d.shape[0]
    tn = 512
    return pl.pallas_call(
        _inproj_kernel,
        grid=(m // tm, n_out // tn),
        in_specs=[
            pl.BlockSpec((tm, D_MODEL), lambda i, j: (i, 0)),
            pl.BlockSpec((1, D_MODEL), lambda i, j: (0, 0)),
            pl.BlockSpec((1, 1, D_MODEL), lambda i, j: (mod_row_of_tile(i), 0, 0)),
            pl.BlockSpec((1, 1, D_MODEL), lambda i, j: (mod_row_of_tile(i), 0, 1)),
            pl.BlockSpec((D_MODEL, tn), lambda i, j: (0, j)),
        ],
        out_specs=pl.BlockSpec((tm, tn), lambda i, j: (i, j)),
        out_shape=jax.ShapeDtypeStruct((m, n_out), F32),
        scratch_shapes=[pltpu.VMEM((tm, D_MODEL), BF16)],
        compiler_params=_cparams(2, VMEM_LIMIT),
        name=name,
    )(x2d, norm_w, mod3, mod3, w_bf16)


def _rope(x, c, sa, sb):
    return x * c + pltpu.roll(x, HEAD_W - 16, 1) * sa + pltpu.roll(x, 16, 1) * sb


def _attn_kernel(lamp_ref, q_ref, k_ref, v_ref, kc_ref, vc_ref, cq_ref, saq_ref, sbq_ref,
                 ck_ref, sak_ref, sbk_ref, subln_ref, o_ref, k_s, v_s):
    tq = q_ref.shape[1]

    @pl.when(pl.program_id(2) == 0)
    def _():
        k = _rope(k_ref[0], ck_ref[...], sak_ref[...], sbk_ref[...])
        k_s[0:SEQ, :] = k.astype(BF16)
        k_s[SEQ:SEQ + CTX_LEN, :] = kc_ref[0].astype(BF16)
        v_s[0:SEQ, :] = v_ref[0].astype(BF16)
        v_s[SEQ:SEQ + CTX_LEN, :] = vc_ref[0].astype(BF16)

    lp = lamp_ref[...]
    lam = (jnp.exp(jnp.sum(lp[0:1] * lp[1:2], axis=-1, keepdims=True))
           - jnp.exp(jnp.sum(lp[2:3] * lp[3:4], axis=-1, keepdims=True)) + LAM_INIT)

    q = _rope(q_ref[0], cq_ref[...], saq_ref[...], sbq_ref[...]) * (HEAD_DIM ** -0.5)
    lane = lax.broadcasted_iota(jnp.int32, q.shape, 1)
    q0 = jnp.where(lane < HEAD_DIM, q, 0.0).astype(BF16)
    q1 = jnp.where(lane >= HEAD_DIM, q, 0.0).astype(BF16)
    qq = jnp.concatenate([q0, q1], axis=0)
    s = lax.dot_general(qq, k_s[...], (((1,), (1,)), ((), ())), preferred_element_type=F32)
    m = jnp.max(s, axis=-1, keepdims=True)
    e = jnp.exp(s - m)
    p = e / jnp.sum(e, axis=-1, keepdims=True)
    a = p[0:tq] - lam * p[tq:2 * tq]
    o = jnp.dot(a.astype(BF16), v_s[...], preferred_element_type=F32)
    o = _rms(o, subln_ref[...]) * (1.0 - LAM_INIT)
    o_ref[0] = o.astype(BF16)


def _attn_call(z3, zc3, lam_params, rope_tabs, subln_w):
    tq = 256
    c_t, sa_t, sb_t = rope_tabs
    qmap = lambda b, h, i: (i, 0)
    full = lambda b, h, i: (0, 0)
    return pl.pallas_call(
        _attn_kernel,
        grid=(BATCH, HEADS, SEQ // tq),
        in_specs=[
            pl.BlockSpec((4, HEAD_DIM), full),
            pl.BlockSpec((1, tq, HEAD_W), lambda b, h, i: (b, i, COL_AQ + h)),
            pl.BlockSpec((1, SEQ, HEAD_W), lambda b, h, i: (b, 0, COL_AK + h)),
            pl.BlockSpec((1, SEQ, HEAD_W), lambda b, h, i: (b, 0, COL_AV + h)),
            pl.BlockSpec((1, CTX_LEN, HEAD_W), lambda b, h, i: (b, 0, COL_AK + h)),
            pl.BlockSpec((1, CTX_LEN, HEAD_W), lambda b, h, i: (b, 0, COL_AV + h)),
            pl.BlockSpec((tq, HEAD_W), qmap),
            pl.BlockSpec((tq, HEAD_W), qmap),
            pl.BlockSpec((tq, HEAD_W), qmap),
            pl.BlockSpec((SEQ, HEAD_W), full),
            pl.BlockSpec((SEQ, HEAD_W), full),
            pl.BlockSpec((SEQ, HEAD_W), full),
            pl.BlockSpec((1, HEAD_W), full),
        ],
        out_specs=pl.BlockSpec((1, tq, HEAD_W), lambda b, h, i: (b, i, h)),
        out_shape=jax.ShapeDtypeStruct((BATCH, SEQ, ATTN_WIDTH), BF16),
        scratch_shapes=[pltpu.VMEM((SEQ + CTX_LEN, HEAD_W), BF16),
                        pltpu.VMEM((SEQ + CTX_LEN, HEAD_W), BF16)],
        compiler_params=_cparams(3, VMEM_LIMIT),
        name="attn",
    )(lam_params, z3, z3, z3, zc3, zc3, c_t, sa_t, sb_t, c_t, sa_t, sb_t, subln_w)


LAT_CHUNKS = SEQ // CHUNK
CTX_CHUNKS = CTX_LEN // CHUNK
PREP_ROWS = 256
INTRA_GROUP = 2
SCAN_UNROLL = 4
LOG2E = 1.4426950408889634


def _tri(rev):
    r = lax.broadcasted_iota(jnp.int32, (CHUNK, CHUNK), 0)
    c = lax.broadcasted_iota(jnp.int32, (CHUNK, CHUNK), 1)
    return jnp.where((c >= r) if rev else (c <= r), 1.0, 0.0).astype(F32)


def _gate(f_raw, lower):
    f = lower + (1.0 - lower) * _sigmoid(f_raw)
    return 1.0 - f, jnp.log(f)


def _prep(f_raw, lower, rev):
    n = f_raw.shape[0] // CHUNK
    k, logf = _gate(f_raw, lower)
    wide = jnp.concatenate([logf[i * CHUNK:(i + 1) * CHUNK] for i in range(n)], axis=1)
    bw = jnp.dot(_tri(rev), wide, precision=lax.Precision.HIGHEST, preferred_element_type=F32)
    t = 0 if rev else CHUNK - 1
    b = jnp.concatenate([bw[:, i * HEAD_W:(i + 1) * HEAD_W] for i in range(n)], axis=0)
    tots = [bw[t:t + 1, i * HEAD_W:(i + 1) * HEAD_W] for i in range(n)]
    b_tot = jnp.concatenate([jnp.broadcast_to(r, (CHUNK, HEAD_W)) for r in tots], axis=0)
    return k, b, b_tot, tots


def _diag_block(q_s, k_d, b_d, v_ref, base, rev):
    rows = lax.broadcasted_iota(jnp.int32, (8, HEAD_W), 0)
    halves = range(SUB // 8)
    qh = [q_s[pl.ds(base + 8 * u, 8), :] for u in halves]
    bh = [b_d[pl.ds(base + 8 * u, 8), :] for u in halves]
    acc = [jnp.zeros((8, HEAD_W), F32) for _ in halves]
    for s in range(SUB):
        bs = b_d[pl.ds(base + s, 1), :]
        ks = k_d[pl.ds(base + s, 1), :]
        vs = v_ref[0, pl.ds(base + s, 1), :]
        for u in halves:
            t_lo, t_hi = 8 * u, 8 * u + 7
            if (t_lo > s) if rev else (t_hi < s):
                continue
            w = jnp.exp2(bh[u] - bs)
            if (t_hi > s) if rev else (t_lo < s):
                keep = (rows + t_lo <= s) if rev else (rows + t_lo >= s)
                w = jnp.where(keep, w, 0.0)
            col = jnp.sum(qh[u] * w * ks, axis=-1, keepdims=True)
            acc[u] = acc[u] + col * vs
    return jnp.concatenate(acc, axis=0)


def _offdiag_scores(q, k, b, rev):
    blocks = []
    for i in range(CHUNK // SUB):
        lo, hi = SUB * i, SUB * (i + 1)
        if rev:
            e_lo, e_hi = hi, CHUNK
            ref_row = b[hi:hi + 1] if hi < CHUNK else None
        else:
            e_lo, e_hi = 0, lo
            ref_row = b[lo - 1:lo] if lo > 0 else None
        if ref_row is None:
            blocks.append(jnp.zeros((SUB, CHUNK), F32))
            continue
        qt = (q[lo:hi] * jnp.exp2(b[lo:hi] - ref_row)).astype(BF16)
        kt = (k[e_lo:e_hi] * jnp.exp2(ref_row - b[e_lo:e_hi])).astype(BF16)
        parts = []
        if e_lo > 0:
            parts.append(jnp.zeros((e_lo, HEAD_W), BF16))
        parts.append(kt)
        if e_hi < CHUNK:
            parts.append(jnp.zeros((CHUNK - e_hi, HEAD_W), BF16))
        ktp = jnp.concatenate(parts, axis=0)
        blocks.append(lax.dot_general(qt, ktp, (((1,), (1,)), ((), ())), preferred_element_type=F32))
    return jnp.concatenate(blocks, axis=0)


def _hgrn2_kernel(lb_ref, ff_ref, fb_ref, v_ref, q_ref, ffc_ref, fbc_ref, vc_ref, o_ref,
                  q_s, k_s, b_s, qbar_s, kbar_s, dec_s, kbarc_s, decc_s):
    lb = lb_ref[...]
    lower = []
    for d in range(2):
        l0, l1 = lb[d, 0:1], lb[d, 1:2]
        mx = jnp.maximum(l0, l1)
        e0, e1 = jnp.exp(l0 - mx), jnp.exp(l1 - mx)
        lower.append(e0 / (e0 + e1))
    f_lat = (ff_ref, fb_ref)
    f_ctx = (ffc_ref, fbc_ref)

    for d in range(2):
        k, b, b_tot, tots = _prep(f_ctx[d][0], lower[d], d == 1)
        kbarc_s[d] = (k * jnp.exp(b_tot - b)).astype(BF16)
        for c in range(CTX_CHUNKS):
            decc_s[d, c] = jnp.broadcast_to(jnp.exp(tots[c]), (8, HEAD_W))

    def prep_body(g, carry):
        rows = pl.ds(pl.multiple_of(g * PREP_ROWS, PREP_ROWS), PREP_ROWS)
        qr = q_ref[0, rows, :]
        q = qr * _sigmoid(qr)
        q_s[rows, :] = q
        for d in range(2):
            k, b, b_tot, tots = _prep(f_lat[d][0, rows, :], lower[d], d == 1)
            k_s[d, rows, :] = k
            b_s[d, rows, :] = b * LOG2E
            qbar_s[d, rows, :] = (q * jnp.exp(b)).astype(BF16)
            kbar_s[d, rows, :] = (k * jnp.exp(b_tot - b)).astype(BF16)
            for i in range(PREP_ROWS // CHUNK):
                dec_s[d, g * (PREP_ROWS // CHUNK) + i] = jnp.broadcast_to(jnp.exp(tots[i]), (8, HEAD_W))
        return carry

    lax.fori_loop(0, SEQ // PREP_ROWS, prep_body, 0)

    def intra_body(g, carry):
        bases = [pl.multiple_of((g * INTRA_GROUP + j) * CHUNK, CHUNK) for j in range(INTRA_GROUP)]
        for base in bases:
            rows = pl.ds(base, CHUNK)
            q = q_s[rows, :]
            v16 = v_ref[0, rows, :].astype(BF16)
            o = None
            for d in range(2):
                a = _offdiag_scores(q, k_s[d, rows, :], b_s[d, rows, :], d == 1)
                oi = jnp.dot(a.astype(BF16), v16, preferred_element_type=F32)
                o = oi if o is None else o + oi
            o_ref[0, rows, :] = o
        for base in bases:
            for i in range(CHUNK // SUB):
                blk = pl.multiple_of(base + SUB * i, SUB)
                od = (_diag_block(q_s, k_s.at[0], b_s.at[0], v_ref, blk, False)
                      + _diag_block(q_s, k_s.at[1], b_s.at[1], v_ref, blk, True))
                o_ref[0, pl.ds(blk, SUB), :] = o_ref[0, pl.ds(blk, SUB), :] + od
        return carry

    lax.fori_loop(0, LAT_CHUNKS // INTRA_GROUP, intra_body, 0)

    def advance(st, dec, v16, kbar):
        upd = lax.dot_general(v16, kbar, (((0,), (0,)), ((), ())), preferred_element_type=F32)
        return dec[0:1] * st + upd

    def ctx_body(c, sts):
        out = []
        for d in range(2):
            cc = (CTX_CHUNKS - 1 - c) if d == 1 else c
            rows = pl.ds(pl.multiple_of(cc * CHUNK, CHUNK), CHUNK)
            out.append(advance(sts[d], decc_s[d, cc], vc_ref[0, rows, :].astype(BF16), kbarc_s[d, rows, :]))
        return tuple(out)

    def lat_body(it, sts):
        sts = list(sts)
        for u in range(SCAN_UNROLL):
            c = it * SCAN_UNROLL + u
            for d in range(2):
                cc = (LAT_CHUNKS - 1 - c) if d == 1 else c
                rows = pl.ds(pl.multiple_of(cc * CHUNK, CHUNK), CHUNK)
                o_ref[0, rows, :] = o_ref[0, rows, :] + lax.dot_general(
                    qbar_s[d, rows, :], sts[d].astype(BF16), (((1,), (1,)), ((), ())),
                    preferred_element_type=F32)
                sts[d] = advance(sts[d], dec_s[d, cc], v_ref[0, rows, :].astype(BF16), kbar_s[d, rows, :])
        return tuple(sts)

    zero = jnp.zeros((HEAD_W, HEAD_W), F32)
    sts = lax.fori_loop(0, CTX_CHUNKS, ctx_body, (zero, zero))
    lax.fori_loop(0, LAT_CHUNKS // SCAN_UNROLL, lat_body, sts)


def _hgrn2_call(z3, zc3, rec_lb):
    lat = lambda col: pl.BlockSpec((1, SEQ, HEAD_W), lambda b, h: (b, 0, col + h))
    ctx = lambda col: pl.BlockSpec((1, CTX_LEN, HEAD_W), lambda b, h: (b, 0, col + h))
    return pl.pallas_call(
        _hgrn2_kernel,
        grid=(BATCH, HEADS),
        in_specs=[
            pl.BlockSpec((2, 2, HEAD_W), lambda b, h: (0, 0, h)),
            lat(COL_RFF), lat(COL_RFB), lat(COL_RI), lat(COL_RQ),
            ctx(COL_RFF), ctx(COL_RFB), ctx(COL_RI),
        ],
        out_specs=pl.BlockSpec((1, SEQ, HEAD_W), lambda b, h: (b, 0, h)),
        out_shape=jax.ShapeDtypeStruct((BATCH, SEQ, REC_WIDTH), F32),
        scratch_shapes=[
            pltpu.VMEM((SEQ, HEAD_W), F32),
            pltpu.VMEM((2, SEQ, HEAD_W), F32),
            pltpu.VMEM((2, SEQ, HEAD_W), F32),
            pltpu.VMEM((2, SEQ, HEAD_W), BF16),
            pltpu.VMEM((2, SEQ, HEAD_W), BF16),
            pltpu.VMEM((2, LAT_CHUNKS, 8, HEAD_W), F32),
            pltpu.VMEM((2, CTX_LEN, HEAD_W), BF16),
            pltpu.VMEM((2, CTX_CHUNKS, 8, HEAD_W), F32),
        ],
        compiler_params=_cparams(2, VMEM_LIMIT),
        name="hgrn2",
    )(rec_lb, z3, z3, z3, z3, zc3, zc3, zc3)


def _merge_kernel(att_ref, orec_ref, rg_ref, ga_ref, gr_ref, x_ref, g1_ref, sh2_ref, sc2_ref,
                  gnw_ref, n2w_ref, wba_ref, wbr_ref, wout_ref, x1_ref, h2_ref):
    rg = rg_ref[...]
    rec = _rms(orec_ref[...], gnw_ref[...]) * (rg * _sigmoid(rg))
    ya = jnp.dot(att_ref[...], wba_ref[...], preferred_element_type=F32)
    yr = jnp.dot(rec.astype(BF16), wbr_ref[...], preferred_element_type=F32)
    y = _sigmoid(ga_ref[...]) * ya + _sigmoid(gr_ref[...]) * yr
    x1 = x_ref[...] + g1_ref[0] * jnp.dot(y.astype(BF16), wout_ref[...], preferred_element_type=F32)
    x1_ref[...] = x1
    h2 = _rms(x1, n2w_ref[...]) * (1.0 + sc2_ref[0]) + sh2_ref[0]
    h2_ref[...] = h2.astype(BF16)


def _merge_call(att2d, orec2d, z2d, x2d, mod3, gnorm_w, norm2_w, wba, wbr, wout):
    tm = 256
    m = x2d.shape[0]
    row = lambda i: (i, 0)
    const = lambda i: (0, 0)
    modspec = lambda k: pl.BlockSpec((1, 1, D_MODEL), lambda i: ((i * tm) // SEQ, 0, k))
    resident = lambda shape: pl.BlockSpec(shape, const, pipeline_mode=pl.Buffered(1))
    return pl.pallas_call(
        _merge_kernel,
        grid=(m // tm,),
        in_specs=[
            pl.BlockSpec((tm, ATTN_WIDTH), row),
            pl.BlockSpec((tm, REC_WIDTH), row),
            pl.BlockSpec((tm, REC_WIDTH), lambda i: (i, 7)),
            pl.BlockSpec((tm, D_MODEL), lambda i: (i, 4)),
            pl.BlockSpec((tm, D_MODEL), lambda i: (i, 5)),
            pl.BlockSpec((tm, D_MODEL), row),
            modspec(2), modspec(3), modspec(4),
            pl.BlockSpec((1, REC_WIDTH), const),
            pl.BlockSpec((1, D_MODEL), const),
            resident((ATTN_WIDTH, D_MODEL)),
            resident((REC_WIDTH, D_MODEL)),
            resident((D_MODEL, D_MODEL)),
        ],
        out_specs=[pl.BlockSpec((tm, D_MODEL), row), pl.BlockSpec((tm, D_MODEL), row)],
        out_shape=[jax.ShapeDtypeStruct((m, D_MODEL), F32), jax.ShapeDtypeStruct((m, D_MODEL), BF16)],
        compiler_params=_cparams(1, VMEM_LIMIT),
        name="merge",
    )(att2d, orec2d, z2d, z2d, z2d, x2d, mod3, mod3, mod3, gnorm_w, norm2_w, wba, wbr, wout)


def _seq_conv(u, cw, cb):
    n = u.shape[0]
    rows = lax.broadcasted_iota(jnp.int32, u.shape, 0)
    prev = jnp.where(rows == 0, 0.0, pltpu.roll(u, 1, 0))
    nxt = jnp.where(rows == n - 1, 0.0, pltpu.roll(u, n - 1, 0))
    return cb + (prev * cw[0:1] + u * cw[1:2] + nxt * cw[2:3])


def _ffn_up_kernel(h_ref, wa_ref, wb_ref, cwa_ref, cwb_ref, cba_ref, cbb_ref, g_ref):
    h = h_ref[0]
    ua = jnp.dot(h, wa_ref[...], preferred_element_type=F32)
    ub = jnp.dot(h, wb_ref[...], preferred_element_type=F32)
    a = _seq_conv(ua, cwa_ref[...], cba_ref[...])
    b = _seq_conv(ub, cwb_ref[...], cbb_ref[...])
    g_ref[0] = (a * _sigmoid(a) * b).astype(BF16)


def _ffn_up_call(h3, w_up, conv_w, conv_b):
    tn = 256
    nb = FFN_DIM // tn
    return pl.pallas_call(
        _ffn_up_kernel,
        grid=(BATCH, nb),
        in_specs=[
            pl.BlockSpec((1, SEQ, D_MODEL), lambda b, j: (b, 0, 0)),
            pl.BlockSpec((D_MODEL, tn), lambda b, j: (0, j)),
            pl.BlockSpec((D_MODEL, tn), lambda b, j: (0, nb + j)),
            pl.BlockSpec((3, tn), lambda b, j: (0, j)),
            pl.BlockSpec((3, tn), lambda b, j: (0, nb + j)),
            pl.BlockSpec((1, tn), lambda b, j: (0, j)),
            pl.BlockSpec((1, tn), lambda b, j: (0, nb + j)),
        ],
        out_specs=pl.BlockSpec((1, SEQ, tn), lambda b, j: (b, 0, j)),
        out_shape=jax.ShapeDtypeStruct((BATCH, SEQ, FFN_DIM), BF16),
        compiler_params=_cparams(2, VMEM_LIMIT),
        name="ffn_up",
    )(h3, w_up, w_up, conv_w, conv_w, conv_b, conv_b)


def _ffn_down_kernel(g_ref, w_ref, x1_ref, g2_ref, fw_ref, o_ref):
    y = jnp.dot(g_ref[...], w_ref[...], preferred_element_type=F32)
    x2 = x1_ref[...] + g2_ref[0] * y
    o_ref[...] = _rms(x2, fw_ref[...])


def _ffn_down_call(g2d, w_down, x1, mod3, final_w):
    tm = 256
    m = g2d.shape[0]
    return pl.pallas_call(
        _ffn_down_kernel,
        grid=(m // tm,),
        in_specs=[
            pl.BlockSpec((tm, FFN_DIM), lambda i: (i, 0)),
            pl.BlockSpec((FFN_DIM, D_MODEL), lambda i: (0, 0), pipeline_mode=pl.Buffered(1)),
            pl.BlockSpec((tm, D_MODEL), lambda i: (i, 0)),
            pl.BlockSpec((1, 1, D_MODEL), lambda i: ((i * tm) // SEQ, 0, 5)),
            pl.BlockSpec((1, D_MODEL), lambda i: (0, 0)),
        ],
        out_specs=pl.BlockSpec((tm, D_MODEL), lambda i: (i, 0)),
        out_shape=jax.ShapeDtypeStruct((m, D_MODEL), F32),
        compiler_params=_cparams(1, VMEM_LIMIT),
        name="ffn_down",
    )(g2d, w_down, x1, mod3, final_w)


def _rope_tables():
    rows = SEQ // GRID_W
    r, col = jnp.meshgrid(jnp.arange(rows), jnp.arange(GRID_W), indexing="ij")
    pos = jnp.stack([r.reshape(-1), col.reshape(-1)], axis=-1).astype(F32)
    nq = HEAD_DIM // 4
    inv = ROPE_BASE ** (-jnp.arange(nq, dtype=F32) / nq)
    ang = pos[:, :, None] * inv
    cos, sin = jnp.cos(ang), jnp.sin(ang)
    lane = jnp.arange(HEAD_W)
    axis = (lane % HEAD_DIM) // (2 * nq)
    second = ((lane % (2 * nq)) // nq) == 1
    freq = lane % nq
    c_t = cos[:, axis, freq]
    s_t = sin[:, axis, freq]
    sa_t = jnp.where(second[None, :], 0.0, -s_t)
    sb_t = jnp.where(second[None, :], s_t, 0.0)
    return c_t, sa_t, sb_t


def kernel(x, c, ctx, c_ctx, w_mod, b_mod, norm1_w, w_in, lam_q1, lam_k1, lam_q2, lam_k2, subln_w,
           rec_lb, rec_gnorm_w, w_branch_attn, w_branch_rec, w_out, norm2_w, w_up, conv_w, conv_b,
           w_down, final_norm_w):
    m_lat = BATCH * SEQ
    c_all = jnp.concatenate([c, c_ctx[None, :], jnp.zeros((MOD_ROWS - BATCH - 1, D_MODEL), F32)], axis=0)
    mod = _mod_call(c_all, w_mod[0], b_mod[0][None, :])
    mod3 = mod.reshape(MOD_ROWS, 1, N_MOD * D_MODEL)

    w_in16 = w_in[0].astype(BF16)
    n1w = norm1_w[0][None, :]
    x2d = x.reshape(m_lat, D_MODEL)
    tm_lat = 1024
    z = _inproj_call(x2d, n1w, mod3, w_in16, IN_WIDTH, lambda i: (i * tm_lat) // SEQ, tm_lat, "inproj")
    zc = _inproj_call(ctx.reshape(BATCH * CTX_LEN, D_MODEL), n1w, mod3, w_in16, CTX_KV_WIDTH,
                      lambda i: CTX_MOD_ROW, 1024, "inproj_ctx")
    z3 = z.reshape(BATCH, SEQ, IN_WIDTH)
    zc3 = zc.reshape(BATCH, CTX_LEN, CTX_KV_WIDTH)

    lam_params = jnp.stack([lam_q1[0], lam_k1[0], lam_q2[0], lam_k2[0]], axis=0)
    att = _attn_call(z3, zc3, lam_params, _rope_tables(), subln_w[0][None, :])
    orec = _hgrn2_call(z3, zc3, rec_lb)

    x1, h2 = _merge_call(att.reshape(m_lat, ATTN_WIDTH), orec.reshape(m_lat, REC_WIDTH), z, x2d, mod3,
                         rec_gnorm_w[0][None, :], norm2_w[0][None, :],
                         w_branch_attn[0].astype(BF16), w_branch_rec[0].astype(BF16), w_out[0].astype(BF16))
    g = _ffn_up_call(h2.reshape(BATCH, SEQ, D_MODEL), w_up[0].astype(BF16), conv_w[0], conv_b[0][None, :])
    out = _ffn_down_call(g.reshape(m_lat, FFN_DIM), w_down[0].astype(BF16), x1, mod3, final_norm_w[None, :])
    return out.reshape(BATCH, SEQ, D_MODEL)
```

```python
import math

import jax
import jax.numpy as jnp
from jax import lax
from jax.experimental import pallas as pl
from jax.experimental.pallas import tpu as pltpu

F32 = jnp.float32
BF16 = jnp.bfloat16

D_MODEL = 2048
BATCH = 8
SEQ = 2048
GRID_W = 64
CTX_LEN = 256
EPS = 1e-6
N_MOD = 6
HEADS = 8
HEAD_DIM = 64
HEAD_W = 128
ATTN_WIDTH = 1024
REC_WIDTH = 1024
FFN_DIM = 5632
IN_WIDTH = 12288
CTX_KV_WIDTH = 5120
ROPE_BASE = 10000.0
LAM_INIT = 0.8 - 0.6 * math.exp(-0.3 * 0)

COL_AK, COL_AV, COL_RFF, COL_RFB, COL_RI, COL_AQ, COL_RQ = 0, 8, 16, 24, 32, 40, 48
MOD_ROWS = 16
CTX_MOD_ROW = BATCH

CHUNK = 64
SUB = 16
VMEM_LIMIT = 56 * 1024 * 1024


def _cparams(n_axes, vmem=None):
    return pltpu.CompilerParams(
        dimension_semantics=("arbitrary",) * n_axes,
        vmem_limit_bytes=vmem,
    )


def _sigmoid(x):
    return 1.0 / (1.0 + jnp.exp(-x))


def _rms(x, w):
    return x * lax.rsqrt(jnp.mean(x * x, axis=-1, keepdims=True) + EPS) * w


def _mod_kernel(c_ref, w_ref, b_ref, o_ref):
    c = c_ref[...]
    a = (c * _sigmoid(c)).astype(BF16)
    o_ref[...] = jnp.dot(a, w_ref[...].astype(BF16), preferred_element_type=F32) + b_ref[...]


def _mod_call(c_all, w_mod, b_mod):
    tn = 1024
    n = w_mod.shape[1]
    return pl.pallas_call(
        _mod_kernel,
        grid=(n // tn,),
        in_specs=[
            pl.BlockSpec((MOD_ROWS, D_MODEL), lambda j: (0, 0)),
            pl.BlockSpec((D_MODEL, tn), lambda j: (0, j)),
            pl.BlockSpec((1, tn), lambda j: (0, j)),
        ],
        out_specs=pl.BlockSpec((MOD_ROWS, tn), lambda j: (0, j)),
        out_shape=jax.ShapeDtypeStruct((MOD_ROWS, n), F32),
        compiler_params=_cparams(1, VMEM_LIMIT),
        name="mod",
    )(c_all, w_mod, b_mod)


def _inproj_kernel(x_ref, nw_ref, sh_ref, sc_ref, w_ref, o_ref, h_ref):
    @pl.when(pl.program_id(1) == 0)
    def _():
        h = _rms(x_ref[...], nw_ref[...]) * (1.0 + sc_ref[0]) + sh_ref[0]
        h_ref[...] = h.astype(BF16)

    o_ref[...] = jnp.dot(h_ref[...], w_ref[...], preferred_element_type=F32)


def _inproj_call(x2d, norm_w, mod3, w_bf16, n_out, mod_row_of_tile, tm, name):
    m = x2d.shape[0]
    tn = 512
    return pl.pallas_call(
        _inproj_kernel,
        grid=(m // tm, n_out // tn),
        in_specs=[
            pl.BlockSpec((tm, D_MODEL), lambda i, j: (i, 0)),
            pl.BlockSpec((1, D_MODEL), lambda i, j: (0, 0)),
            pl.BlockSpec((1, 1, D_MODEL), lambda i, j: (mod_row_of_tile(i), 0, 0)),
            pl.BlockSpec((1, 1, D_MODEL), lambda i, j: (mod_row_of_tile(i), 0, 1)),
            pl.BlockSpec((D_MODEL, tn), lambda i, j: (0, j)),
        ],
        out_specs=pl.BlockSpec((tm, tn), lambda i, j: (i, j)),
        out_shape=jax.ShapeDtypeStruct((m, n_out), F32),
        scratch_shapes=[pltpu.VMEM((tm, D_MODEL), BF16)],
        compiler_params=_cparams(2, VMEM_LIMIT),
        name=name,
    )(x2d, norm_w, mod3, mod3, w_bf16)


def _rope(x, c, sa, sb):
    return x * c + pltpu.roll(x, HEAD_W - 16, 1) * sa + pltpu.roll(x, 16, 1) * sb


def _attn_kernel(lamp_ref, q_ref, k_ref, v_ref, kc_ref, vc_ref, cq_ref, saq_ref, sbq_ref,
                 ck_ref, sak_ref, sbk_ref, subln_ref, o_ref, k_s, v_s):
    tq = q_ref.shape[1]
    n_sub = tq // ATTN_SUB

    @pl.when(pl.program_id(2) == 0)
    def _():
        k = _rope(k_ref[0], ck_ref[...], sak_ref[...], sbk_ref[...])
        k_s[0:SEQ, :] = k.astype(BF16)
        k_s[SEQ:SEQ + CTX_LEN, :] = kc_ref[0].astype(BF16)
        v_s[0:SEQ, 0:HEAD_W] = v_ref[0].astype(BF16)
        v_s[SEQ:SEQ + CTX_LEN, 0:HEAD_W] = vc_ref[0].astype(BF16)
        v_s[:, HEAD_W:2 * HEAD_W] = jnp.ones((SEQ + CTX_LEN, HEAD_W), BF16)

    lp = lamp_ref[...]
    lam = (jnp.exp(jnp.sum(lp[0:1] * lp[1:2], axis=-1, keepdims=True))
           - jnp.exp(jnp.sum(lp[2:3] * lp[3:4], axis=-1, keepdims=True)) + LAM_INIT)

    q = _rope(q_ref[0], cq_ref[...], saq_ref[...], sbq_ref[...]) * (HEAD_DIM ** -0.5 * LOG2E)
    lane = lax.broadcasted_iota(jnp.int32, q.shape, 1)
    q0 = jnp.where(lane < HEAD_DIM, q, 0.0).astype(BF16)
    q1 = jnp.where(lane >= HEAD_DIM, q, 0.0).astype(BF16)

    def scores(i):
        lo, hi = i * ATTN_SUB, (i + 1) * ATTN_SUB
        qq = jnp.concatenate([q0[lo:hi], q1[lo:hi]], axis=0)
        return lax.dot_general(qq, k_s[...], (((1,), (1,)), ((), ())), preferred_element_type=F32)

    s_next = scores(0)
    for i in range(n_sub):
        s = s_next
        if i + 1 < n_sub:
            s_next = scores(i + 1)
        e = jnp.exp2(s - jnp.max(s, axis=-1, keepdims=True)).astype(BF16)
        r = jnp.dot(e, v_s[...], preferred_element_type=F32)
        p = r[:, 0:HEAD_W] / r[:, HEAD_W:2 * HEAD_W]
        o = p[0:ATTN_SUB] - lam * p[ATTN_SUB:2 * ATTN_SUB]
        o = _rms(o, subln_ref[...]) * (1.0 - LAM_INIT)
        o_ref[0, i * ATTN_SUB:(i + 1) * ATTN_SUB, :] = o.astype(BF16)


ATTN_SUB = 128


def _attn_call(z3, zc3, lam_params, rope_tabs, subln_w):
    tq = 1024
    c_t, sa_t, sb_t = rope_tabs
    qmap = lambda b, h, i: (i, 0)
    full = lambda b, h, i: (0, 0)
    return pl.pallas_call(
        _attn_kernel,
        grid=(BATCH, HEADS, SEQ // tq),
        in_specs=[
            pl.BlockSpec((4, HEAD_DIM), full),
            pl.BlockSpec((1, tq, HEAD_W), lambda b, h, i: (b, i, COL_AQ + h)),
            pl.BlockSpec((1, SEQ, HEAD_W), lambda b, h, i: (b, 0, COL_AK + h)),
            pl.BlockSpec((1, SEQ, HEAD_W), lambda b, h, i: (b, 0, COL_AV + h)),
            pl.BlockSpec((1, CTX_LEN, HEAD_W), lambda b, h, i: (b, 0, COL_AK + h)),
            pl.BlockSpec((1, CTX_LEN, HEAD_W), lambda b, h, i: (b, 0, COL_AV + h)),
            pl.BlockSpec((tq, HEAD_W), qmap),
            pl.BlockSpec((tq, HEAD_W), qmap),
            pl.BlockSpec((tq, HEAD_W), qmap),
            pl.BlockSpec((SEQ, HEAD_W), full),
            pl.BlockSpec((SEQ, HEAD_W), full),
            pl.BlockSpec((SEQ, HEAD_W), full),
            pl.BlockSpec((1, HEAD_W), full),
        ],
        out_specs=pl.BlockSpec((1, tq, HEAD_W), lambda b, h, i: (b, i, h)),
        out_shape=jax.ShapeDtypeStruct((BATCH, SEQ, ATTN_WIDTH), BF16),
        scratch_shapes=[pltpu.VMEM((SEQ + CTX_LEN, HEAD_W), BF16),
                        pltpu.VMEM((SEQ + CTX_LEN, 2 * HEAD_W), BF16)],
        compiler_params=_cparams(3, VMEM_LIMIT),
        name="attn",
    )(lam_params, z3, z3, z3, zc3, zc3, c_t, sa_t, sb_t, c_t, sa_t, sb_t, subln_w)


LAT_CHUNKS = SEQ // CHUNK
CTX_CHUNKS = CTX_LEN // CHUNK
PREP_ROWS = 256
INTRA_GROUP = 2
SCAN_UNROLL = 16
LOG2E = 1.4426950408889634


def _tri(rev):
    r = lax.broadcasted_iota(jnp.int32, (CHUNK, CHUNK), 0)
    c = lax.broadcasted_iota(jnp.int32, (CHUNK, CHUNK), 1)
    return jnp.where((c >= r) if rev else (c <= r), 1.0, 0.0).astype(F32)


def _gate(f_raw, lower):
    f = lower + (1.0 - lower) * _sigmoid(f_raw)
    return 1.0 - f, jnp.log(f)


def _prep(f_raw, lower, rev):
    n = f_raw.shape[0] // CHUNK
    k, logf = _gate(f_raw, lower)
    wide = jnp.concatenate([logf[i * CHUNK:(i + 1) * CHUNK] for i in range(n)], axis=1)
    bw = jnp.dot(_tri(rev), wide, precision=lax.Precision.HIGHEST, preferred_element_type=F32)
    t = 0 if rev else CHUNK - 1
    b = jnp.concatenate([bw[:, i * HEAD_W:(i + 1) * HEAD_W] for i in range(n)], axis=0)
    tots = [bw[t:t + 1, i * HEAD_W:(i + 1) * HEAD_W] for i in range(n)]
    b_tot = jnp.concatenate([jnp.broadcast_to(r, (CHUNK, HEAD_W)) for r in tots], axis=0)
    return k, b, b_tot, tots


def _diag_block(q_s, k_d, b_d, v_ref, base, rev):
    rows = lax.broadcasted_iota(jnp.int32, (8, HEAD_W), 0)
    halves = range(SUB // 8)
    qh = [q_s[pl.ds(base + 8 * u, 8), :] for u in halves]
    bh = [b_d[pl.ds(base + 8 * u, 8), :] for u in halves]
    acc = [jnp.zeros((8, HEAD_W), F32) for _ in halves]
    for s in range(SUB):
        bs = b_d[pl.ds(base + s, 1), :]
        ks = k_d[pl.ds(base + s, 1), :]
        vs = v_ref[0, pl.ds(base + s, 1), :]
        for u in halves:
            t_lo, t_hi = 8 * u, 8 * u + 7
            if (t_lo > s) if rev else (t_hi < s):
                continue
            w = jnp.exp2(bh[u] - bs)
            if (t_hi > s) if rev else (t_lo < s):
                keep = (rows + t_lo <= s) if rev else (rows + t_lo >= s)
                w = jnp.where(keep, w, 0.0)
            col = jnp.sum(qh[u] * w * ks, axis=-1, keepdims=True)
            acc[u] = acc[u] + col * vs
    return jnp.concatenate(acc, axis=0)


def _offdiag_scores(q, k, b, rev):
    blocks = []
    for i in range(CHUNK // SUB):
        lo, hi = SUB * i, SUB * (i + 1)
        if rev:
            e_lo, e_hi = hi, CHUNK
            ref_row = b[hi:hi + 1] if hi < CHUNK else None
        else:
            e_lo, e_hi = 0, lo
            ref_row = b[lo - 1:lo] if lo > 0 else None
        if ref_row is None:
            blocks.append(jnp.zeros((SUB, CHUNK), F32))
            continue
        qt = (q[lo:hi] * jnp.exp2(b[lo:hi] - ref_row)).astype(BF16)
        kt = (k[e_lo:e_hi] * jnp.exp2(ref_row - b[e_lo:e_hi])).astype(BF16)
        parts = []
        if e_lo > 0:
            parts.append(jnp.zeros((e_lo, HEAD_W), BF16))
        parts.append(kt)
        if e_hi < CHUNK:
            parts.append(jnp.zeros((CHUNK - e_hi, HEAD_W), BF16))
        ktp = jnp.concatenate(parts, axis=0)
        blocks.append(lax.dot_general(qt, ktp, (((1,), (1,)), ((), ())), preferred_element_type=F32))
    return jnp.concatenate(blocks, axis=0)


def _hgrn2_kernel(lb_ref, ff_ref, fb_ref, v_ref, q_ref, ffc_ref, fbc_ref, vc_ref, o_ref,
                  q_s, k_s, b_s, qbar_s, kbar_s, dec_s, kbarc_s, decc_s, ob_s):
    lb = lb_ref[...]
    lower = []
    for d in range(2):
        l0, l1 = lb[d, 0:1], lb[d, 1:2]
        mx = jnp.maximum(l0, l1)
        e0, e1 = jnp.exp(l0 - mx), jnp.exp(l1 - mx)
        lower.append(e0 / (e0 + e1))
    f_lat = (ff_ref, fb_ref)
    f_ctx = (ffc_ref, fbc_ref)

    for d in range(2):
        k, b, b_tot, tots = _prep(f_ctx[d][0], lower[d], d == 1)
        kbarc_s[d] = (k * jnp.exp(b_tot - b)).astype(BF16)
        for c in range(CTX_CHUNKS):
            decc_s[d, c] = jnp.broadcast_to(jnp.exp(tots[c]), (8, HEAD_W))

    def prep_body(g, carry):
        rows = pl.ds(pl.multiple_of(g * PREP_ROWS, PREP_ROWS), PREP_ROWS)
        qr = q_ref[0, rows, :]
        q = qr * _sigmoid(qr)
        q_s[rows, :] = q
        for d in range(2):
            k, b, b_tot, tots = _prep(f_lat[d][0, rows, :], lower[d], d == 1)
            k_s[d, rows, :] = k
            b_s[d, rows, :] = b * LOG2E
            qbar_s[d, rows, :] = (q * jnp.exp(b)).astype(BF16)
            kbar_s[d, rows, :] = (k * jnp.exp(b_tot - b)).astype(BF16)
            for i in range(PREP_ROWS // CHUNK):
                dec_s[d, g * (PREP_ROWS // CHUNK) + i] = jnp.broadcast_to(jnp.exp(tots[i]), (8, HEAD_W))
        return carry

    lax.fori_loop(0, SEQ // PREP_ROWS, prep_body, 0)

    def intra_body(g, carry):
        bases = [pl.multiple_of((g * INTRA_GROUP + j) * CHUNK, CHUNK) for j in range(INTRA_GROUP)]

        def diag(base):
            blocks = []
            for i in range(CHUNK // SUB):
                blk = pl.multiple_of(base + SUB * i, SUB)
                blocks.append(_diag_block(q_s, k_s.at[0], b_s.at[0], v_ref, blk, False)
                              + _diag_block(q_s, k_s.at[1], b_s.at[1], v_ref, blk, True))
            return jnp.concatenate(blocks, axis=0)

        scores = [[_offdiag_scores(q_s[pl.ds(base, CHUNK), :], k_s[d, pl.ds(base, CHUNK), :],
                                   b_s[d, pl.ds(base, CHUNK), :], d == 1).astype(BF16) for d in range(2)]
                  for base in bases]
        od = [diag(bases[0])]
        pv = []
        for j, base in enumerate(bases):
            v16 = v_ref[0, pl.ds(base, CHUNK), :].astype(BF16)
            pv.append(jnp.dot(scores[j][0], v16, preferred_element_type=F32)
                      + jnp.dot(scores[j][1], v16, preferred_element_type=F32))
        od += [diag(base) for base in bases[1:]]
        for j, base in enumerate(bases):
            rows = pl.ds(base, CHUNK)
            o_ref[0, rows, :] = (o_ref[0, rows, :] + ob_s[rows, :]) + (pv[j] + od[j])
        return carry

    def advance(st, dec, v16, kbar):
        upd = lax.dot_general(v16, kbar, (((0,), (0,)), ((), ())), preferred_element_type=F32)
        return dec[0:1] * st + upd

    def ctx_body(c, sts):
        out = []
        for d in range(2):
            cc = (CTX_CHUNKS - 1 - c) if d == 1 else c
            rows = pl.ds(pl.multiple_of(cc * CHUNK, CHUNK), CHUNK)
            out.append(advance(sts[d], decc_s[d, cc], vc_ref[0, rows, :].astype(BF16), kbarc_s[d, rows, :]))
        return tuple(out)

    def lat_body(it, sts):
        sts = list(sts)
        steps = []
        for u in range(SCAN_UNROLL):
            c = it * SCAN_UNROLL + u
            for d in range(2):
                cc = (LAT_CHUNKS - 1 - c) if d == 1 else c
                steps.append((d, cc, pl.ds(pl.multiple_of(cc * CHUNK, CHUNK), CHUNK)))
        upd = [lax.dot_general(v_ref[0, rows, :].astype(BF16), kbar_s[d, rows, :], (((0,), (0,)), ((), ())),
                               preferred_element_type=F32) for d, cc, rows in steps]
        for n, (d, cc, rows) in enumerate(steps):
            inter = lax.dot_general(qbar_s[d, rows, :], sts[d].astype(BF16), (((1,), (1,)), ((), ())),
                                    preferred_element_type=F32)
            if d == 0:
                o_ref[0, rows, :] = inter
            else:
                ob_s[rows, :] = inter
            sts[d] = dec_s[d, cc][0:1] * sts[d] + upd[n]
        return tuple(sts)

    zero = jnp.zeros((HEAD_W, HEAD_W), F32)
    sts = lax.fori_loop(0, CTX_CHUNKS, ctx_body, (zero, zero))
    lax.fori_loop(0, LAT_CHUNKS // SCAN_UNROLL, lat_body, sts)
    lax.fori_loop(0, LAT_CHUNKS // INTRA_GROUP, intra_body, 0)


def _hgrn2_call(z3, zc3, rec_lb):
    lat = lambda col: pl.BlockSpec((1, SEQ, HEAD_W), lambda b, h: (b, 0, col + h))
    ctx = lambda col: pl.BlockSpec((1, CTX_LEN, HEAD_W), lambda b, h: (b, 0, col + h))
    return pl.pallas_call(
        _hgrn2_kernel,
        grid=(BATCH, HEADS),
        in_specs=[
            pl.BlockSpec((2, 2, HEAD_W), lambda b, h: (0, 0, h)),
            lat(COL_RFF), lat(COL_RFB), lat(COL_RI), lat(COL_RQ),
            ctx(COL_RFF), ctx(COL_RFB), ctx(COL_RI),
        ],
        out_specs=pl.BlockSpec((1, SEQ, HEAD_W), lambda b, h: (b, 0, h)),
        out_shape=jax.ShapeDtypeStruct((BATCH, SEQ, REC_WIDTH), F32),
        scratch_shapes=[
            pltpu.VMEM((SEQ, HEAD_W), F32),
            pltpu.VMEM((2, SEQ, HEAD_W), F32),
            pltpu.VMEM((2, SEQ, HEAD_W), F32),
            pltpu.VMEM((2, SEQ, HEAD_W), BF16),
            pltpu.VMEM((2, SEQ, HEAD_W), BF16),
            pltpu.VMEM((2, LAT_CHUNKS, 8, HEAD_W), F32),
            pltpu.VMEM((2, CTX_LEN, HEAD_W), BF16),
            pltpu.VMEM((2, CTX_CHUNKS, 8, HEAD_W), F32),
            pltpu.VMEM((SEQ, HEAD_W), F32),
        ],
        compiler_params=_cparams(2, VMEM_LIMIT),
        name="hgrn2",
    )(rec_lb, z3, z3, z3, z3, zc3, zc3, zc3)


def _merge_kernel(att_ref, orec_ref, rg_ref, ga_ref, gr_ref, x_ref, g1_ref, sh2_ref, sc2_ref,
                  gnw_ref, n2w_ref, wba_ref, wbr_ref, wout_ref, x1_ref, h2_ref):
    rg = rg_ref[...]
    rec = _rms(orec_ref[...], gnw_ref[...]) * (rg * _sigmoid(rg))
    ya = jnp.dot(att_ref[...], wba_ref[...], preferred_element_type=F32)
    yr = jnp.dot(rec.astype(BF16), wbr_ref[...], preferred_element_type=F32)
    y = _sigmoid(ga_ref[...]) * ya + _sigmoid(gr_ref[...]) * yr
    x1 = x_ref[...] + g1_ref[0] * jnp.dot(y.astype(BF16), wout_ref[...], preferred_element_type=F32)
    x1_ref[...] = x1
    h2 = _rms(x1, n2w_ref[...]) * (1.0 + sc2_ref[0]) + sh2_ref[0]
    h2_ref[...] = h2.astype(BF16)


def _merge_call(att2d, orec2d, z2d, x2d, mod3, gnorm_w, norm2_w, wba, wbr, wout):
    tm = 256
    m = x2d.shape[0]
    row = lambda i: (i, 0)
    const = lambda i: (0, 0)
    modspec = lambda k: pl.BlockSpec((1, 1, D_MODEL), lambda i: ((i * tm) // SEQ, 0, k))
    resident = lambda shape: pl.BlockSpec(shape, const, pipeline_mode=pl.Buffered(1))
    return pl.pallas_call(
        _merge_kernel,
        grid=(m // tm,),
        in_specs=[
            pl.BlockSpec((tm, ATTN_WIDTH), row),
            pl.BlockSpec((tm, REC_WIDTH), row),
            pl.BlockSpec((tm, REC_WIDTH), lambda i: (i, 7)),
            pl.BlockSpec((tm, D_MODEL), lambda i: (i, 4)),
            pl.BlockSpec((tm, D_MODEL), lambda i: (i, 5)),
            pl.BlockSpec((tm, D_MODEL), row),
            modspec(2), modspec(3), modspec(4),
            pl.BlockSpec((1, REC_WIDTH), const),
            pl.BlockSpec((1, D_MODEL), const),
            resident((ATTN_WIDTH, D_MODEL)),
            resident((REC_WIDTH, D_MODEL)),
            resident((D_MODEL, D_MODEL)),
        ],
        out_specs=[pl.BlockSpec((tm, D_MODEL), row), pl.BlockSpec((tm, D_MODEL), row)],
        out_shape=[jax.ShapeDtypeStruct((m, D_MODEL), F32), jax.ShapeDtypeStruct((m, D_MODEL), BF16)],
        compiler_params=_cparams(1, VMEM_LIMIT),
        name="merge",
    )(att2d, orec2d, z2d, z2d, z2d, x2d, mod3, mod3, mod3, gnorm_w, norm2_w, wba, wbr, wout)


FFN_COLS = 256


def _seq_conv(u, cw, cb):
    n = u.shape[0]
    u3 = u.reshape(n // 8, 8, u.shape[1])
    sub = lax.broadcasted_iota(jnp.int32, u3.shape, 1)
    zero_grp = jnp.zeros((1,) + u3.shape[1:], F32)
    down = pltpu.roll(u3, 1, 1)
    up = pltpu.roll(u3, 7, 1)
    prev = jnp.where(sub == 0, jnp.concatenate([zero_grp, down[:-1]], axis=0), down)
    nxt = jnp.where(sub == 7, jnp.concatenate([up[1:], zero_grp], axis=0), up)
    return (cb + (prev * cw[0:1] + u3 * cw[1:2] + nxt * cw[2:3])).reshape(u.shape)


def _ffn_up_kernel(h_ref, wa_ref, wb_ref, cwa_ref, cwb_ref, cba_ref, cbb_ref, g_ref):
    h = h_ref[0]
    n_units = wa_ref.shape[1] // FFN_COLS
    units = []
    for c in range(n_units):
        cols = slice(c * FFN_COLS, (c + 1) * FFN_COLS)
        units.append((wa_ref, cwa_ref, cba_ref, cols))
        units.append((wb_ref, cwb_ref, cbb_ref, cols))

    def proj(unit):
        w_ref, _, _, cols = unit
        return jnp.dot(h, w_ref[:, cols], preferred_element_type=F32)

    def conv(unit, u):
        _, cw_ref, cb_ref, cols = unit
        return _seq_conv(u, cw_ref[:, cols], cb_ref[:, cols])

    u = proj(units[0])
    a = None
    for n, unit in enumerate(units):
        u_next = proj(units[n + 1]) if n + 1 < len(units) else None
        y = conv(unit, u)
        if n % 2 == 0:
            a = y * _sigmoid(y)
        else:
            g_ref[0, :, unit[3]] = (a * y).astype(BF16)
        u = u_next


def _ffn_up_call(h3, w_up, conv_w, conv_b):
    tn = 512
    nb = FFN_DIM // tn
    return pl.pallas_call(
        _ffn_up_kernel,
        grid=(BATCH, nb),
        in_specs=[
            pl.BlockSpec((1, SEQ, D_MODEL), lambda b, j: (b, 0, 0)),
            pl.BlockSpec((D_MODEL, tn), lambda b, j: (0, j)),
            pl.BlockSpec((D_MODEL, tn), lambda b, j: (0, nb + j)),
            pl.BlockSpec((3, tn), lambda b, j: (0, j)),
            pl.BlockSpec((3, tn), lambda b, j: (0, nb + j)),
            pl.BlockSpec((1, tn), lambda b, j: (0, j)),
            pl.BlockSpec((1, tn), lambda b, j: (0, nb + j)),
        ],
        out_specs=pl.BlockSpec((1, SEQ, tn), lambda b, j: (b, 0, j)),
        out_shape=jax.ShapeDtypeStruct((BATCH, SEQ, FFN_DIM), BF16),
        compiler_params=_cparams(2, VMEM_LIMIT),
        name="ffn_up",
    )(h3, w_up, w_up, conv_w, conv_w, conv_b, conv_b)


def _ffn_down_kernel(g_ref, w_ref, x1_ref, g2_ref, fw_ref, o_ref):
    y = jnp.dot(g_ref[...], w_ref[...], preferred_element_type=F32)
    x2 = x1_ref[...] + g2_ref[0] * y
    o_ref[...] = _rms(x2, fw_ref[...])


def _ffn_down_call(g2d, w_down, x1, mod3, final_w):
    tm = 256
    m = g2d.shape[0]
    return pl.pallas_call(
        _ffn_down_kernel,
        grid=(m // tm,),
        in_specs=[
            pl.BlockSpec((tm, FFN_DIM), lambda i: (i, 0)),
            pl.BlockSpec((FFN_DIM, D_MODEL), lambda i: (0, 0), pipeline_mode=pl.Buffered(1)),
            pl.BlockSpec((tm, D_MODEL), lambda i: (i, 0)),
            pl.BlockSpec((1, 1, D_MODEL), lambda i: ((i * tm) // SEQ, 0, 5)),
            pl.BlockSpec((1, D_MODEL), lambda i: (0, 0)),
        ],
        out_specs=pl.BlockSpec((tm, D_MODEL), lambda i: (i, 0)),
        out_shape=jax.ShapeDtypeStruct((m, D_MODEL), F32),
        compiler_params=_cparams(1, VMEM_LIMIT),
        name="ffn_down",
    )(g2d, w_down, x1, mod3, final_w)


def _rope_tables():
    rows = SEQ // GRID_W
    r, col = jnp.meshgrid(jnp.arange(rows), jnp.arange(GRID_W), indexing="ij")
    pos = jnp.stack([r.reshape(-1), col.reshape(-1)], axis=-1).astype(F32)
    nq = HEAD_DIM // 4
    inv = ROPE_BASE ** (-jnp.arange(nq, dtype=F32) / nq)
    ang = pos[:, :, None] * inv
    cos, sin = jnp.cos(ang), jnp.sin(ang)
    lane = jnp.arange(HEAD_W)
    axis = (lane % HEAD_DIM) // (2 * nq)
    second = ((lane % (2 * nq)) // nq) == 1
    freq = lane % nq
    c_t = cos[:, axis, freq]
    s_t = sin[:, axis, freq]
    sa_t = jnp.where(second[None, :], 0.0, -s_t)
    sb_t = jnp.where(second[None, :], s_t, 0.0)
    return c_t, sa_t, sb_t


def kernel(x, c, ctx, c_ctx, w_mod, b_mod, norm1_w, w_in, lam_q1, lam_k1, lam_q2, lam_k2, subln_w,
           rec_lb, rec_gnorm_w, w_branch_attn, w_branch_rec, w_out, norm2_w, w_up, conv_w, conv_b,
           w_down, final_norm_w):
    m_lat = BATCH * SEQ
    c_all = jnp.concatenate([c, c_ctx[None, :], jnp.zeros((MOD_ROWS - BATCH - 1, D_MODEL), F32)], axis=0)
    mod = _mod_call(c_all, w_mod[0], b_mod[0][None, :])
    mod3 = mod.reshape(MOD_ROWS, 1, N_MOD * D_MODEL)

    w_in16 = w_in[0].astype(BF16)
    n1w = norm1_w[0][None, :]
    x2d = x.reshape(m_lat, D_MODEL)
    tm_lat = 1024
    z = _inproj_call(x2d, n1w, mod3, w_in16, IN_WIDTH, lambda i: (i * tm_lat) // SEQ, tm_lat, "inproj")
    zc = _inproj_call(ctx.reshape(BATCH * CTX_LEN, D_MODEL), n1w, mod3, w_in16, CTX_KV_WIDTH,
                      lambda i: CTX_MOD_ROW, 1024, "inproj_ctx")
    z3 = z.reshape(BATCH, SEQ, IN_WIDTH)
    zc3 = zc.reshape(BATCH, CTX_LEN, CTX_KV_WIDTH)

    lam_params = jnp.stack([lam_q1[0], lam_k1[0], lam_q2[0], lam_k2[0]], axis=0)
    att = _attn_call(z3, zc3, lam_params, _rope_tables(), subln_w[0][None, :])
    orec = _hgrn2_call(z3, zc3, rec_lb)

    x1, h2 = _merge_call(att.reshape(m_lat, ATTN_WIDTH), orec.reshape(m_lat, REC_WIDTH), z, x2d, mod3,
                         rec_gnorm_w[0][None, :], norm2_w[0][None, :],
                         w_branch_attn[0].astype(BF16), w_branch_rec[0].astype(BF16), w_out[0].astype(BF16))
    g = _ffn_up_call(h2.reshape(BATCH, SEQ, D_MODEL), w_up[0].astype(BF16), conv_w[0], conv_b[0][None, :])
    out = _ffn_down_call(g.reshape(m_lat, FFN_DIM), w_down[0].astype(BF16), x1, mod3, final_norm_w[None, :])
    return out.reshape(BATCH, SEQ, D_MODEL)
```

```python
import math

import jax
import jax.numpy as jnp
from jax import lax
from jax.experimental import pallas as pl
from jax.experimental.pallas import tpu as pltpu

F32 = jnp.float32
BF16 = jnp.bfloat16

D_MODEL = 2048
BATCH = 8
SEQ = 2048
GRID_W = 64
CTX_LEN = 256
EPS = 1e-6
N_MOD = 6
HEADS = 8
HEAD_DIM = 64
HEAD_W = 128
ATTN_WIDTH = 1024
REC_WIDTH = 1024
FFN_DIM = 5632
IN_WIDTH = 12288
CTX_KV_WIDTH = 5120
ROPE_BASE = 10000.0
LAM_INIT = 0.8 - 0.6 * math.exp(-0.3 * 0)

COL_AK, COL_AV, COL_RFF, COL_RFB, COL_RI, COL_AQ, COL_RQ = 0, 8, 16, 24, 32, 40, 48
MOD_ROWS = 16
CTX_MOD_ROW = BATCH

CHUNK = 64
SUB = 16
VMEM_LIMIT = 56 * 1024 * 1024


def _cparams(n_axes, vmem=None):
    return pltpu.CompilerParams(
        dimension_semantics=("arbitrary",) * n_axes,
        vmem_limit_bytes=vmem,
    )


def _sigmoid(x):
    return 1.0 / (1.0 + jnp.exp(-x))


def _rms(x, w):
    return x * lax.rsqrt(jnp.mean(x * x, axis=-1, keepdims=True) + EPS) * w


def _mod_kernel(c_ref, w_ref, b_ref, o_ref):
    c = c_ref[...]
    a = (c * _sigmoid(c)).astype(BF16)
    o_ref[...] = jnp.dot(a, w_ref[...].astype(BF16), preferred_element_type=F32) + b_ref[...]


def _mod_call(c_all, w_mod, b_mod):
    tn = 1024
    n = w_mod.shape[1]
    return pl.pallas_call(
        _mod_kernel,
        grid=(n // tn,),
        in_specs=[
            pl.BlockSpec((MOD_ROWS, D_MODEL), lambda j: (0, 0)),
            pl.BlockSpec((D_MODEL, tn), lambda j: (0, j)),
            pl.BlockSpec((1, tn), lambda j: (0, j)),
        ],
        out_specs=pl.BlockSpec((MOD_ROWS, tn), lambda j: (0, j)),
        out_shape=jax.ShapeDtypeStruct((MOD_ROWS, n), F32),
        compiler_params=_cparams(1, VMEM_LIMIT),
        name="mod",
    )(c_all, w_mod, b_mod)


def _inproj_kernel(x_ref, nw_ref, sh_ref, sc_ref, w_ref, o_ref, h_ref):
    @pl.when(pl.program_id(1) == 0)
    def _():
        h = _rms(x_ref[...], nw_ref[...]) * (1.0 + sc_ref[0]) + sh_ref[0]
        h_ref[...] = h.astype(BF16)

    o_ref[...] = jnp.dot(h_ref[...], w_ref[...], preferred_element_type=F32)


def _inproj_call(x2d, norm_w, mod3, w_bf16, n_out, mod_row_of_tile, tm, name):
    m = x2d.shape[0]
    tn = 1024
    return pl.pallas_call(
        _inproj_kernel,
        grid=(m // tm, n_out // tn),
        in_specs=[
            pl.BlockSpec((tm, D_MODEL), lambda i, j: (i, 0)),
            pl.BlockSpec((1, D_MODEL), lambda i, j: (0, 0)),
            pl.BlockSpec((1, 1, D_MODEL), lambda i, j: (mod_row_of_tile(i), 0, 0)),
            pl.BlockSpec((1, 1, D_MODEL), lambda i, j: (mod_row_of_tile(i), 0, 1)),
            pl.BlockSpec((D_MODEL, tn), lambda i, j: (0, j)),
        ],
        out_specs=pl.BlockSpec((tm, tn), lambda i, j: (i, j)),
        out_shape=jax.ShapeDtypeStruct((m, n_out), F32),
        scratch_shapes=[pltpu.VMEM((tm, D_MODEL), BF16)],
        compiler_params=_cparams(2, VMEM_LIMIT),
        name=name,
    )(x2d, norm_w, mod3, mod3, w_bf16)


def _rope(x, c, sa, sb):
    return x * c + pltpu.roll(x, HEAD_W - 16, 1) * sa + pltpu.roll(x, 16, 1) * sb


def _attn_kernel(lamp_ref, q_ref, k_ref, v_ref, kc_ref, vc_ref, cq_ref, saq_ref, sbq_ref,
                 ck_ref, sak_ref, sbk_ref, subln_ref, o_ref, k_s, v_s):
    tq = q_ref.shape[1]
    n_sub = tq // ATTN_SUB

    @pl.when(pl.program_id(2) == 0)
    def _():
        k = _rope(k_ref[0], ck_ref[...], sak_ref[...], sbk_ref[...])
        k_s[0:SEQ, :] = k.astype(BF16)
        k_s[SEQ:SEQ + CTX_LEN, :] = kc_ref[0].astype(BF16)
        v_s[0:SEQ, 0:HEAD_W] = v_ref[0].astype(BF16)
        v_s[SEQ:SEQ + CTX_LEN, 0:HEAD_W] = vc_ref[0].astype(BF16)
        v_s[:, HEAD_W:2 * HEAD_W] = jnp.ones((SEQ + CTX_LEN, HEAD_W), BF16)

    lp = lamp_ref[...]
    lam = (jnp.exp(jnp.sum(lp[0:1] * lp[1:2], axis=-1, keepdims=True))
           - jnp.exp(jnp.sum(lp[2:3] * lp[3:4], axis=-1, keepdims=True)) + LAM_INIT)

    q = _rope(q_ref[0], cq_ref[...], saq_ref[...], sbq_ref[...]) * (HEAD_DIM ** -0.5 * LOG2E)
    lane = lax.broadcasted_iota(jnp.int32, q.shape, 1)
    q0 = jnp.where(lane < HEAD_DIM, q, 0.0).astype(BF16)
    q1 = jnp.where(lane >= HEAD_DIM, q, 0.0).astype(BF16)

    def scores(i):
        lo, hi = i * ATTN_SUB, (i + 1) * ATTN_SUB
        qq = jnp.concatenate([q0[lo:hi], q1[lo:hi]], axis=0)
        return lax.dot_general(qq, k_s[...], (((1,), (1,)), ((), ())), preferred_element_type=F32)

    s_next = scores(0)
    for i in range(n_sub):
        s = s_next
        if i + 1 < n_sub:
            s_next = scores(i + 1)
        e = jnp.exp2(s - jnp.max(s, axis=-1, keepdims=True)).astype(BF16)
        r = jnp.dot(e, v_s[...], preferred_element_type=F32)
        p = r[:, 0:HEAD_W] / r[:, HEAD_W:2 * HEAD_W]
        o = p[0:ATTN_SUB] - lam * p[ATTN_SUB:2 * ATTN_SUB]
        o = _rms(o, subln_ref[...]) * (1.0 - LAM_INIT)
        o_ref[0, i * ATTN_SUB:(i + 1) * ATTN_SUB, :] = o.astype(BF16)


ATTN_SUB = 128


def _attn_call(z3, zc3, lam_params, rope_tabs, subln_w):
    tq = 1024
    c_t, sa_t, sb_t = rope_tabs
    qmap = lambda b, h, i: (i, 0)
    full = lambda b, h, i: (0, 0)
    return pl.pallas_call(
        _attn_kernel,
        grid=(BATCH, HEADS, SEQ // tq),
        in_specs=[
            pl.BlockSpec((4, HEAD_DIM), full),
            pl.BlockSpec((1, tq, HEAD_W), lambda b, h, i: (b, i, COL_AQ + h)),
            pl.BlockSpec((1, SEQ, HEAD_W), lambda b, h, i: (b, 0, COL_AK + h)),
            pl.BlockSpec((1, SEQ, HEAD_W), lambda b, h, i: (b, 0, COL_AV + h)),
            pl.BlockSpec((1, CTX_LEN, HEAD_W), lambda b, h, i: (b, 0, COL_AK + h)),
            pl.BlockSpec((1, CTX_LEN, HEAD_W), lambda b, h, i: (b, 0, COL_AV + h)),
            pl.BlockSpec((tq, HEAD_W), qmap),
            pl.BlockSpec((tq, HEAD_W), qmap),
            pl.BlockSpec((tq, HEAD_W), qmap),
            pl.BlockSpec((SEQ, HEAD_W), full),
            pl.BlockSpec((SEQ, HEAD_W), full),
            pl.BlockSpec((SEQ, HEAD_W), full),
            pl.BlockSpec((1, HEAD_W), full),
        ],
        out_specs=pl.BlockSpec((1, tq, HEAD_W), lambda b, h, i: (b, i, h)),
        out_shape=jax.ShapeDtypeStruct((BATCH, SEQ, ATTN_WIDTH), BF16),
        scratch_shapes=[pltpu.VMEM((SEQ + CTX_LEN, HEAD_W), BF16),
                        pltpu.VMEM((SEQ + CTX_LEN, 2 * HEAD_W), BF16)],
        compiler_params=_cparams(3, VMEM_LIMIT),
        name="attn",
    )(lam_params, z3, z3, z3, zc3, zc3, c_t, sa_t, sb_t, c_t, sa_t, sb_t, subln_w)


LAT_CHUNKS = SEQ // CHUNK
CTX_CHUNKS = CTX_LEN // CHUNK
PREP_ROWS = 256
INTRA_GROUP = 2
BOUNDED_GROUP = 8
DIAG_LOG2_BOUND = 96.0
SCAN_UNROLL = 16
LOG2E = 1.4426950408889634


def _tri(rev):
    r = lax.broadcasted_iota(jnp.int32, (CHUNK, CHUNK), 0)
    c = lax.broadcasted_iota(jnp.int32, (CHUNK, CHUNK), 1)
    return jnp.where((c >= r) if rev else (c <= r), 1.0, 0.0).astype(F32)


def _gate(f_raw, lower):
    f = lower + (1.0 - lower) * _sigmoid(f_raw)
    return 1.0 - f, jnp.log(f)


def _prep(f_raw, lower, rev):
    n = f_raw.shape[0] // CHUNK
    k, logf = _gate(f_raw, lower)
    wide = jnp.concatenate([logf[i * CHUNK:(i + 1) * CHUNK] for i in range(n)], axis=1)
    bw = jnp.dot(_tri(rev), wide, precision=lax.Precision.HIGHEST, preferred_element_type=F32)
    t = 0 if rev else CHUNK - 1
    b = jnp.concatenate([bw[:, i * HEAD_W:(i + 1) * HEAD_W] for i in range(n)], axis=0)
    tots = [bw[t:t + 1, i * HEAD_W:(i + 1) * HEAD_W] for i in range(n)]
    b_tot = jnp.concatenate([jnp.broadcast_to(r, (CHUNK, HEAD_W)) for r in tots], axis=0)
    return k, b, b_tot, tots


def _diag_block(q_s, k_d, b_d, v_ref, base, rev):
    rows = lax.broadcasted_iota(jnp.int32, (8, HEAD_W), 0)
    halves = range(SUB // 8)
    qh = [q_s[pl.ds(base + 8 * u, 8), :] for u in halves]
    bh = [b_d[pl.ds(base + 8 * u, 8), :] for u in halves]
    acc = [jnp.zeros((8, HEAD_W), F32) for _ in halves]
    for s in range(SUB):
        bs = b_d[pl.ds(base + s, 1), :]
        ks = k_d[pl.ds(base + s, 1), :]
        vs = v_ref[0, pl.ds(base + s, 1), :]
        for u in halves:
            t_lo, t_hi = 8 * u, 8 * u + 7
            if (t_lo > s) if rev else (t_hi < s):
                continue
            w = jnp.exp2(bh[u] - bs)
            if (t_hi > s) if rev else (t_lo < s):
                keep = (rows + t_lo <= s) if rev else (rows + t_lo >= s)
                w = jnp.where(keep, w, 0.0)
            col = jnp.sum(qh[u] * w * ks, axis=-1, keepdims=True)
            acc[u] = acc[u] + col * vs
    return jnp.concatenate(acc, axis=0)


def _block_scores(q, k, b, rev, with_diag):
    blocks = []
    for i in range(CHUNK // SUB):
        lo, hi = SUB * i, SUB * (i + 1)
        if rev:
            e_lo, e_hi = (lo if with_diag else hi), CHUNK
            ref_row = b[hi:hi + 1] if hi < CHUNK else None
        else:
            e_lo, e_hi = 0, (hi if with_diag else lo)
            ref_row = b[lo - 1:lo] if lo > 0 else None
        if ref_row is None:
            if not with_diag:
                blocks.append(jnp.zeros((SUB, CHUNK), F32))
                continue
            ref_row = jnp.zeros((1, HEAD_W), F32)
        qt = (q[lo:hi] * jnp.exp2(b[lo:hi] - ref_row)).astype(BF16)
        kt = (k[e_lo:e_hi] * jnp.exp2(ref_row - b[e_lo:e_hi])).astype(BF16)
        parts = []
        if e_lo > 0:
            parts.append(jnp.zeros((e_lo, HEAD_W), BF16))
        parts.append(kt)
        if e_hi < CHUNK:
            parts.append(jnp.zeros((CHUNK - e_hi, HEAD_W), BF16))
        ktp = jnp.concatenate(parts, axis=0)
        blocks.append(lax.dot_general(qt, ktp, (((1,), (1,)), ((), ())), preferred_element_type=F32))
    return jnp.concatenate(blocks, axis=0)


def _hgrn2_kernel(lb_ref, ff_ref, fb_ref, v_ref, q_ref, ffc_ref, fbc_ref, vc_ref, o_ref,
                  q_s, k_s, b_s, qbar_s, kbar_s, dec_s, kbarc_s, decc_s, ob_s):
    lb = lb_ref[...]
    lower = []
    for d in range(2):
        l0, l1 = lb[d, 0:1], lb[d, 1:2]
        mx = jnp.maximum(l0, l1)
        e0, e1 = jnp.exp(l0 - mx), jnp.exp(l1 - mx)
        lower.append(e0 / (e0 + e1))
    f_lat = (ff_ref, fb_ref)
    f_ctx = (ffc_ref, fbc_ref)

    for d in range(2):
        k, b, b_tot, tots = _prep(f_ctx[d][0], lower[d], d == 1)
        kbarc_s[d] = (k * jnp.exp(b_tot - b)).astype(BF16)
        for c in range(CTX_CHUNKS):
            decc_s[d, c] = jnp.broadcast_to(jnp.exp(tots[c]), (8, HEAD_W))

    def prep_body(g, carry):
        rows = pl.ds(pl.multiple_of(g * PREP_ROWS, PREP_ROWS), PREP_ROWS)
        qr = q_ref[0, rows, :]
        q = qr * _sigmoid(qr)
        q_s[rows, :] = q
        for d in range(2):
            k, b, b_tot, tots = _prep(f_lat[d][0, rows, :], lower[d], d == 1)
            k_s[d, rows, :] = k
            b_s[d, rows, :] = b * LOG2E
            qbar_s[d, rows, :] = (q * jnp.exp(b)).astype(BF16)
            kbar_s[d, rows, :] = (k * jnp.exp(b_tot - b)).astype(BF16)
            for i in range(PREP_ROWS // CHUNK):
                dec_s[d, g * (PREP_ROWS // CHUNK) + i] = jnp.broadcast_to(jnp.exp(tots[i]), (8, HEAD_W))
        return carry

    lax.fori_loop(0, SEQ // PREP_ROWS, prep_body, 0)

    def chunk_scores(base, with_diag):
        rows = pl.ds(base, CHUNK)
        out = []
        for d in range(2):
            a = _block_scores(q_s[rows, :], k_s[d, rows, :], b_s[d, rows, :], d == 1, with_diag)
            if with_diag:
                r = lax.broadcasted_iota(jnp.int32, a.shape, 0)
                c = lax.broadcasted_iota(jnp.int32, a.shape, 1)
                a = jnp.where((c >= r) if d == 1 else (c <= r), a, 0.0)
            out.append(a.astype(BF16))
        return out

    def value_products(bases, scores):
        pv = []
        for j, base in enumerate(bases):
            v16 = v_ref[0, pl.ds(base, CHUNK), :].astype(BF16)
            pv.append(jnp.dot(scores[j][0], v16, preferred_element_type=F32)
                      + jnp.dot(scores[j][1], v16, preferred_element_type=F32))
        return pv

    def intra_body(g, carry):
        bases = [pl.multiple_of((g * INTRA_GROUP + j) * CHUNK, CHUNK) for j in range(INTRA_GROUP)]

        def diag(base):
            blocks = []
            for i in range(CHUNK // SUB):
                blk = pl.multiple_of(base + SUB * i, SUB)
                blocks.append(_diag_block(q_s, k_s.at[0], b_s.at[0], v_ref, blk, False)
                              + _diag_block(q_s, k_s.at[1], b_s.at[1], v_ref, blk, True))
            return jnp.concatenate(blocks, axis=0)

        scores = [chunk_scores(base, False) for base in bases]
        od = [diag(bases[0])]
        pv = value_products(bases, scores)
        od += [diag(base) for base in bases[1:]]
        for j, base in enumerate(bases):
            rows = pl.ds(base, CHUNK)
            o_ref[0, rows, :] = (o_ref[0, rows, :] + ob_s[rows, :]) + (pv[j] + od[j])
        return carry

    def intra_body_bounded(g, carry):
        bases = [pl.multiple_of((g * BOUNDED_GROUP + j) * CHUNK, CHUNK) for j in range(BOUNDED_GROUP)]
        pv = value_products(bases, [chunk_scores(base, True) for base in bases])
        for j, base in enumerate(bases):
            rows = pl.ds(base, CHUNK)
            o_ref[0, rows, :] = (o_ref[0, rows, :] + ob_s[rows, :]) + pv[j]
        return carry

    def advance(st, dec, v16, kbar):
        upd = lax.dot_general(v16, kbar, (((0,), (0,)), ((), ())), preferred_element_type=F32)
        return dec[0:1] * st + upd

    def ctx_body(c, sts):
        out = []
        for d in range(2):
            cc = (CTX_CHUNKS - 1 - c) if d == 1 else c
            rows = pl.ds(pl.multiple_of(cc * CHUNK, CHUNK), CHUNK)
            out.append(advance(sts[d], decc_s[d, cc], vc_ref[0, rows, :].astype(BF16), kbarc_s[d, rows, :]))
        return tuple(out)

    def lat_body(it, sts):
        sts = list(sts)
        steps = []
        for u in range(SCAN_UNROLL):
            c = it * SCAN_UNROLL + u
            for d in range(2):
                cc = (LAT_CHUNKS - 1 - c) if d == 1 else c
                steps.append((d, cc, pl.ds(pl.multiple_of(cc * CHUNK, CHUNK), CHUNK)))
        upd = [lax.dot_general(v_ref[0, rows, :].astype(BF16), kbar_s[d, rows, :], (((0,), (0,)), ((), ())),
                               preferred_element_type=F32) for d, cc, rows in steps]
        for n, (d, cc, rows) in enumerate(steps):
            inter = lax.dot_general(qbar_s[d, rows, :], sts[d].astype(BF16), (((1,), (1,)), ((), ())),
                                    preferred_element_type=F32)
            if d == 0:
                o_ref[0, rows, :] = inter
            else:
                ob_s[rows, :] = inter
            sts[d] = dec_s[d, cc][0:1] * sts[d] + upd[n]
        return tuple(sts)

    zero = jnp.zeros((HEAD_W, HEAD_W), F32)
    sts = lax.fori_loop(0, CTX_CHUNKS, ctx_body, (zero, zero))
    lax.fori_loop(0, LAT_CHUNKS // SCAN_UNROLL, lat_body, sts)
    worst = jnp.min(jnp.minimum(lower[0], lower[1]))
    bounded = worst >= 2.0 ** (-DIAG_LOG2_BOUND / SUB)

    @pl.when(bounded)
    def _():
        lax.fori_loop(0, LAT_CHUNKS // BOUNDED_GROUP, intra_body_bounded, 0)

    @pl.when(jnp.logical_not(bounded))
    def _():
        lax.fori_loop(0, LAT_CHUNKS // INTRA_GROUP, intra_body, 0)


def _hgrn2_call(z3, zc3, rec_lb):
    lat = lambda col: pl.BlockSpec((1, SEQ, HEAD_W), lambda b, h: (b, 0, col + h))
    ctx = lambda col: pl.BlockSpec((1, CTX_LEN, HEAD_W), lambda b, h: (b, 0, col + h))
    return pl.pallas_call(
        _hgrn2_kernel,
        grid=(BATCH, HEADS),
        in_specs=[
            pl.BlockSpec((2, 2, HEAD_W), lambda b, h: (0, 0, h)),
            lat(COL_RFF), lat(COL_RFB), lat(COL_RI), lat(COL_RQ),
            ctx(COL_RFF), ctx(COL_RFB), ctx(COL_RI),
        ],
        out_specs=pl.BlockSpec((1, SEQ, HEAD_W), lambda b, h: (b, 0, h)),
        out_shape=jax.ShapeDtypeStruct((BATCH, SEQ, REC_WIDTH), F32),
        scratch_shapes=[
            pltpu.VMEM((SEQ, HEAD_W), F32),
            pltpu.VMEM((2, SEQ, HEAD_W), F32),
            pltpu.VMEM((2, SEQ, HEAD_W), F32),
            pltpu.VMEM((2, SEQ, HEAD_W), BF16),
            pltpu.VMEM((2, SEQ, HEAD_W), BF16),
            pltpu.VMEM((2, LAT_CHUNKS, 8, HEAD_W), F32),
            pltpu.VMEM((2, CTX_LEN, HEAD_W), BF16),
            pltpu.VMEM((2, CTX_CHUNKS, 8, HEAD_W), F32),
            pltpu.VMEM((SEQ, HEAD_W), F32),
        ],
        compiler_params=_cparams(2, VMEM_LIMIT),
        name="hgrn2",
    )(rec_lb, z3, z3, z3, z3, zc3, zc3, zc3)


def _merge_kernel(att_ref, orec_ref, rg_ref, ga_ref, gr_ref, x_ref, g1_ref, sh2_ref, sc2_ref,
                  gnw_ref, n2w_ref, wba_ref, wbr_ref, wout_ref, x1_ref, h2_ref):
    rg = rg_ref[...]
    rec = _rms(orec_ref[...], gnw_ref[...]) * (rg * _sigmoid(rg))
    ya = jnp.dot(att_ref[...], wba_ref[...], preferred_element_type=F32)
    yr = jnp.dot(rec.astype(BF16), wbr_ref[...], preferred_element_type=F32)
    y = _sigmoid(ga_ref[...]) * ya + _sigmoid(gr_ref[...]) * yr
    x1 = x_ref[...] + g1_ref[0] * jnp.dot(y.astype(BF16), wout_ref[...], preferred_element_type=F32)
    x1_ref[...] = x1
    h2 = _rms(x1, n2w_ref[...]) * (1.0 + sc2_ref[0]) + sh2_ref[0]
    h2_ref[...] = h2.astype(BF16)


def _merge_call(att2d, orec2d, z2d, x2d, mod3, gnorm_w, norm2_w, wba, wbr, wout):
    tm = 256
    m = x2d.shape[0]
    row = lambda i: (i, 0)
    const = lambda i: (0, 0)
    modspec = lambda k: pl.BlockSpec((1, 1, D_MODEL), lambda i: ((i * tm) // SEQ, 0, k))
    resident = lambda shape: pl.BlockSpec(shape, const, pipeline_mode=pl.Buffered(1))
    return pl.pallas_call(
        _merge_kernel,
        grid=(m // tm,),
        in_specs=[
            pl.BlockSpec((tm, ATTN_WIDTH), row),
            pl.BlockSpec((tm, REC_WIDTH), row),
            pl.BlockSpec((tm, REC_WIDTH), lambda i: (i, 7)),
            pl.BlockSpec((tm, D_MODEL), lambda i: (i, 4)),
            pl.BlockSpec((tm, D_MODEL), lambda i: (i, 5)),
            pl.BlockSpec((tm, D_MODEL), row),
            modspec(2), modspec(3), modspec(4),
            pl.BlockSpec((1, REC_WIDTH), const),
            pl.BlockSpec((1, D_MODEL), const),
            resident((ATTN_WIDTH, D_MODEL)),
            resident((REC_WIDTH, D_MODEL)),
            resident((D_MODEL, D_MODEL)),
        ],
        out_specs=[pl.BlockSpec((tm, D_MODEL), row), pl.BlockSpec((tm, D_MODEL), row)],
        out_shape=[jax.ShapeDtypeStruct((m, D_MODEL), F32), jax.ShapeDtypeStruct((m, D_MODEL), BF16)],
        compiler_params=_cparams(1, VMEM_LIMIT),
        name="merge",
    )(att2d, orec2d, z2d, z2d, z2d, x2d, mod3, mod3, mod3, gnorm_w, norm2_w, wba, wbr, wout)


FFN_COLS = 256
FFN_PAD = 8


def _ffn_up_kernel(h_ref, wa_ref, wb_ref, cwa_ref, cwb_ref, cba_ref, cbb_ref, g_ref, u_s):
    h = h_ref[0]
    n_units = wa_ref.shape[1] // FFN_COLS
    units = []
    for c in range(n_units):
        cols = slice(c * FFN_COLS, (c + 1) * FFN_COLS)
        units.append((wa_ref, cwa_ref, cba_ref, cols))
        units.append((wb_ref, cwb_ref, cbb_ref, cols))

    pad = jnp.zeros((FFN_PAD, FFN_COLS), F32)
    for slot in range(2):
        u_s[slot, 0:FFN_PAD, :] = pad
        u_s[slot, FFN_PAD + SEQ:2 * FFN_PAD + SEQ, :] = pad

    def proj(n):
        w_ref, _, _, cols = units[n]
        u_s[n % 2, FFN_PAD:FFN_PAD + SEQ, :] = jnp.dot(h, w_ref[:, cols], preferred_element_type=F32)

    def conv(n):
        _, cw_ref, cb_ref, cols = units[n]
        cw = cw_ref[:, cols]
        buf = u_s.at[n % 2]
        return cb_ref[:, cols] + (buf[FFN_PAD - 1:FFN_PAD - 1 + SEQ, :] * cw[0:1]
                                  + buf[FFN_PAD:FFN_PAD + SEQ, :] * cw[1:2]
                                  + buf[FFN_PAD + 1:FFN_PAD + 1 + SEQ, :] * cw[2:3])

    proj(0)
    a = None
    for n in range(len(units)):
        if n + 1 < len(units):
            proj(n + 1)
        y = conv(n)
        if n % 2 == 0:
            a = y * _sigmoid(y)
        else:
            g_ref[0, :, units[n][3]] = (a * y).astype(BF16)


def _ffn_up_call(h3, w_up, conv_w, conv_b):
    tn = 512
    nb = FFN_DIM // tn
    return pl.pallas_call(
        _ffn_up_kernel,
        grid=(BATCH, nb),
        in_specs=[
            pl.BlockSpec((1, SEQ, D_MODEL), lambda b, j: (b, 0, 0)),
            pl.BlockSpec((D_MODEL, tn), lambda b, j: (0, j)),
            pl.BlockSpec((D_MODEL, tn), lambda b, j: (0, nb + j)),
            pl.BlockSpec((3, tn), lambda b, j: (0, j)),
            pl.BlockSpec((3, tn), lambda b, j: (0, nb + j)),
            pl.BlockSpec((1, tn), lambda b, j: (0, j)),
            pl.BlockSpec((1, tn), lambda b, j: (0, nb + j)),
        ],
        out_specs=pl.BlockSpec((1, SEQ, tn), lambda b, j: (b, 0, j)),
        out_shape=jax.ShapeDtypeStruct((BATCH, SEQ, FFN_DIM), BF16),
        scratch_shapes=[pltpu.VMEM((2, SEQ + 2 * FFN_PAD, FFN_COLS), F32)],
        compiler_params=_cparams(2, VMEM_LIMIT),
        name="ffn_up",
    )(h3, w_up, w_up, conv_w, conv_w, conv_b, conv_b)


def _ffn_down_kernel(g_ref, w_ref, x1_ref, g2_ref, fw_ref, o_ref):
    y = jnp.dot(g_ref[...], w_ref[...], preferred_element_type=F32)
    x2 = x1_ref[...] + g2_ref[0] * y
    o_ref[...] = _rms(x2, fw_ref[...])


def _ffn_down_call(g2d, w_down, x1, mod3, final_w):
    tm = 256
    m = g2d.shape[0]
    return pl.pallas_call(
        _ffn_down_kernel,
        grid=(m // tm,),
        in_specs=[
            pl.BlockSpec((tm, FFN_DIM), lambda i: (i, 0)),
            pl.BlockSpec((FFN_DIM, D_MODEL), lambda i: (0, 0), pipeline_mode=pl.Buffered(1)),
            pl.BlockSpec((tm, D_MODEL), lambda i: (i, 0)),
            pl.BlockSpec((1, 1, D_MODEL), lambda i: ((i * tm) // SEQ, 0, 5)),
            pl.BlockSpec((1, D_MODEL), lambda i: (0, 0)),
        ],
        out_specs=pl.BlockSpec((tm, D_MODEL), lambda i: (i, 0)),
        out_shape=jax.ShapeDtypeStruct((m, D_MODEL), F32),
        compiler_params=_cparams(1, VMEM_LIMIT),
        name="ffn_down",
    )(g2d, w_down, x1, mod3, final_w)


def _rope_tables():
    rows = SEQ // GRID_W
    r, col = jnp.meshgrid(jnp.arange(rows), jnp.arange(GRID_W), indexing="ij")
    pos = jnp.stack([r.reshape(-1), col.reshape(-1)], axis=-1).astype(F32)
    nq = HEAD_DIM // 4
    inv = ROPE_BASE ** (-jnp.arange(nq, dtype=F32) / nq)
    ang = pos[:, :, None] * inv
    cos, sin = jnp.cos(ang), jnp.sin(ang)
    lane = jnp.arange(HEAD_W)
    axis = (lane % HEAD_DIM) // (2 * nq)
    second = ((lane % (2 * nq)) // nq) == 1
    freq = lane % nq
    c_t = cos[:, axis, freq]
    s_t = sin[:, axis, freq]
    sa_t = jnp.where(second[None, :], 0.0, -s_t)
    sb_t = jnp.where(second[None, :], s_t, 0.0)
    return c_t, sa_t, sb_t


def kernel(x, c, ctx, c_ctx, w_mod, b_mod, norm1_w, w_in, lam_q1, lam_k1, lam_q2, lam_k2, subln_w,
           rec_lb, rec_gnorm_w, w_branch_attn, w_branch_rec, w_out, norm2_w, w_up, conv_w, conv_b,
           w_down, final_norm_w):
    m_lat = BATCH * SEQ
    c_all = jnp.concatenate([c, c_ctx[None, :], jnp.zeros((MOD_ROWS - BATCH - 1, D_MODEL), F32)], axis=0)
    mod = _mod_call(c_all, w_mod[0], b_mod[0][None, :])
    mod3 = mod.reshape(MOD_ROWS, 1, N_MOD * D_MODEL)

    w_in16 = w_in[0].astype(BF16)
    n1w = norm1_w[0][None, :]
    x2d = x.reshape(m_lat, D_MODEL)
    tm_lat = 1024
    z = _inproj_call(x2d, n1w, mod3, w_in16, IN_WIDTH, lambda i: (i * tm_lat) // SEQ, tm_lat, "inproj")
    zc = _inproj_call(ctx.reshape(BATCH * CTX_LEN, D_MODEL), n1w, mod3, w_in16, CTX_KV_WIDTH,
                      lambda i: CTX_MOD_ROW, 1024, "inproj_ctx")
    z3 = z.reshape(BATCH, SEQ, IN_WIDTH)
    zc3 = zc.reshape(BATCH, CTX_LEN, CTX_KV_WIDTH)

    lam_params = jnp.stack([lam_q1[0], lam_k1[0], lam_q2[0], lam_k2[0]], axis=0)
    att = _attn_call(z3, zc3, lam_params, _rope_tables(), subln_w[0][None, :])
    orec = _hgrn2_call(z3, zc3, rec_lb)

    x1, h2 = _merge_call(att.reshape(m_lat, ATTN_WIDTH), orec.reshape(m_lat, REC_WIDTH), z, x2d, mod3,
                         rec_gnorm_w[0][None, :], norm2_w[0][None, :],
                         w_branch_attn[0].astype(BF16), w_branch_rec[0].astype(BF16), w_out[0].astype(BF16))
    g = _ffn_up_call(h2.reshape(BATCH, SEQ, D_MODEL), w_up[0].astype(BF16), conv_w[0], conv_b[0][None, :])
    out = _ffn_down_call(g.reshape(m_lat, FFN_DIM), w_down[0].astype(BF16), x1, mod3, final_norm_w[None, :])
    return out.reshape(BATCH, SEQ, D_MODEL)
```

```python
import math

import jax
import jax.numpy as jnp
from jax import lax
from jax.experimental import pallas as pl
from jax.experimental.pallas import tpu as pltpu

F32 = jnp.float32
BF16 = jnp.bfloat16

D_MODEL = 2048
BATCH = 8
SEQ = 2048
GRID_W = 64
CTX_LEN = 256
EPS = 1e-6
N_MOD = 6
HEADS = 8
HEAD_DIM = 64
HEAD_W = 128
ATTN_WIDTH = 1024
REC_WIDTH = 1024
FFN_DIM = 5632
IN_WIDTH = 12288
CTX_KV_WIDTH = 5120
ROPE_BASE = 10000.0
LAM_INIT = 0.8 - 0.6 * math.exp(-0.3 * 0)

COL_AK, COL_AV, COL_RFF, COL_RFB, COL_RI, COL_AQ, COL_RQ = 0, 8, 16, 24, 32, 40, 48
MOD_ROWS = 16
CTX_MOD_ROW = BATCH

CHUNK = 64
SUB = 16
VMEM_LIMIT = 56 * 1024 * 1024
LOG2E = 1.4426950408889634


def _cparams(n_axes, vmem=None):
    return pltpu.CompilerParams(
        dimension_semantics=("arbitrary",) * n_axes,
        vmem_limit_bytes=vmem,
    )


def _sigmoid(x):
    return 1.0 / (1.0 + jnp.exp2(x * (-LOG2E)))


def _rms(x, w):
    return x * lax.rsqrt(jnp.mean(x * x, axis=-1, keepdims=True) + EPS) * w


def _mod_kernel(c_ref, w_ref, b_ref, o_ref):
    c = c_ref[...]
    a = (c * _sigmoid(c)).astype(BF16)
    o_ref[...] = jnp.dot(a, w_ref[...].astype(BF16), preferred_element_type=F32) + b_ref[...]


def _mod_call(c_all, w_mod, b_mod):
    tn = 1024
    n = w_mod.shape[1]
    return pl.pallas_call(
        _mod_kernel,
        grid=(n // tn,),
        in_specs=[
            pl.BlockSpec((MOD_ROWS, D_MODEL), lambda j: (0, 0)),
            pl.BlockSpec((D_MODEL, tn), lambda j: (0, j)),
            pl.BlockSpec((1, tn), lambda j: (0, j)),
        ],
        out_specs=pl.BlockSpec((MOD_ROWS, tn), lambda j: (0, j)),
        out_shape=jax.ShapeDtypeStruct((MOD_ROWS, n), F32),
        compiler_params=_cparams(1, VMEM_LIMIT),
        name="mod",
    )(c_all, w_mod, b_mod)


def _inproj_kernel(x_ref, nw_ref, sh_ref, sc_ref, w_ref, o_ref, h_ref):
    @pl.when(pl.program_id(1) == 0)
    def _():
        h = _rms(x_ref[...], nw_ref[...]) * (1.0 + sc_ref[0]) + sh_ref[0]
        h_ref[...] = h.astype(BF16)

    o_ref[...] = jnp.dot(h_ref[...], w_ref[...], preferred_element_type=F32)


def _inproj_call(x2d, norm_w, mod3, w_bf16, n_out, mod_row_of_tile, tm, name):
    m = x2d.shape[0]
    tn = 1024
    return pl.pallas_call(
        _inproj_kernel,
        grid=(m // tm, n_out // tn),
        in_specs=[
            pl.BlockSpec((tm, D_MODEL), lambda i, j: (i, 0)),
            pl.BlockSpec((1, D_MODEL), lambda i, j: (0, 0)),
            pl.BlockSpec((1, 1, D_MODEL), lambda i, j: (mod_row_of_tile(i), 0, 0)),
            pl.BlockSpec((1, 1, D_MODEL), lambda i, j: (mod_row_of_tile(i), 0, 1)),
            pl.BlockSpec((D_MODEL, tn), lambda i, j: (0, j)),
        ],
        out_specs=pl.BlockSpec((tm, tn), lambda i, j: (i, j)),
        out_shape=jax.ShapeDtypeStruct((m, n_out), F32),
        scratch_shapes=[pltpu.VMEM((tm, D_MODEL), BF16)],
        compiler_params=_cparams(2, VMEM_LIMIT),
        name=name,
    )(x2d, norm_w, mod3, mod3, w_bf16)


def _rope(x, c, sa, sb):
    return x * c + pltpu.roll(x, HEAD_W - 16, 1) * sa + pltpu.roll(x, 16, 1) * sb


def _attn_kernel(lamp_ref, q_ref, k_ref, v_ref, kc_ref, vc_ref, cq_ref, saq_ref, sbq_ref,
                 ck_ref, sak_ref, sbk_ref, subln_ref, o_ref, k_s, v_s):
    tq = q_ref.shape[1]
    n_sub = tq // ATTN_SUB

    @pl.when(pl.program_id(2) == 0)
    def _():
        k = _rope(k_ref[0], ck_ref[...], sak_ref[...], sbk_ref[...])
        k_s[0:SEQ, :] = k.astype(BF16)
        k_s[SEQ:SEQ + CTX_LEN, :] = kc_ref[0].astype(BF16)
        v_s[0:SEQ, 0:HEAD_W] = v_ref[0].astype(BF16)
        v_s[SEQ:SEQ + CTX_LEN, 0:HEAD_W] = vc_ref[0].astype(BF16)
        v_s[:, HEAD_W:2 * HEAD_W] = jnp.ones((SEQ + CTX_LEN, HEAD_W), BF16)

    lp = lamp_ref[...]
    lam = (jnp.exp(jnp.sum(lp[0:1] * lp[1:2], axis=-1, keepdims=True))
           - jnp.exp(jnp.sum(lp[2:3] * lp[3:4], axis=-1, keepdims=True)) + LAM_INIT)

    q = _rope(q_ref[0], cq_ref[...], saq_ref[...], sbq_ref[...]) * (HEAD_DIM ** -0.5 * LOG2E)
    lane = lax.broadcasted_iota(jnp.int32, q.shape, 1)
    q0 = jnp.where(lane < HEAD_DIM, q, 0.0).astype(BF16)
    q1 = jnp.where(lane >= HEAD_DIM, q, 0.0).astype(BF16)

    def scores(i):
        lo, hi = i * ATTN_SUB, (i + 1) * ATTN_SUB
        qq = jnp.concatenate([q0[lo:hi], q1[lo:hi]], axis=0)
        return lax.dot_general(qq, k_s[...], (((1,), (1,)), ((), ())), preferred_element_type=F32)

    s_next = scores(0)
    for i in range(n_sub):
        s = s_next
        if i + 1 < n_sub:
            s_next = scores(i + 1)
        e = jnp.exp2(s - jnp.max(s, axis=-1, keepdims=True)).astype(BF16)
        r = jnp.dot(e, v_s[...], preferred_element_type=F32)
        p = r[:, 0:HEAD_W] / r[:, HEAD_W:2 * HEAD_W]
        o = p[0:ATTN_SUB] - lam * p[ATTN_SUB:2 * ATTN_SUB]
        o = _rms(o, subln_ref[...]) * (1.0 - LAM_INIT)
        o_ref[0, i * ATTN_SUB:(i + 1) * ATTN_SUB, :] = o.astype(BF16)


ATTN_SUB = 256


def _attn_call(z3, zc3, lam_params, rope_tabs, subln_w):
    tq = 2048
    c_t, sa_t, sb_t = rope_tabs
    qmap = lambda b, h, i: (i, 0)
    full = lambda b, h, i: (0, 0)
    return pl.pallas_call(
        _attn_kernel,
        grid=(BATCH, HEADS, SEQ // tq),
        in_specs=[
            pl.BlockSpec((4, HEAD_DIM), full),
            pl.BlockSpec((1, tq, HEAD_W), lambda b, h, i: (b, i, COL_AQ + h)),
            pl.BlockSpec((1, SEQ, HEAD_W), lambda b, h, i: (b, 0, COL_AK + h)),
            pl.BlockSpec((1, SEQ, HEAD_W), lambda b, h, i: (b, 0, COL_AV + h)),
            pl.BlockSpec((1, CTX_LEN, HEAD_W), lambda b, h, i: (b, 0, COL_AK + h)),
            pl.BlockSpec((1, CTX_LEN, HEAD_W), lambda b, h, i: (b, 0, COL_AV + h)),
            pl.BlockSpec((tq, HEAD_W), qmap),
            pl.BlockSpec((tq, HEAD_W), qmap),
            pl.BlockSpec((tq, HEAD_W), qmap),
            pl.BlockSpec((SEQ, HEAD_W), full),
            pl.BlockSpec((SEQ, HEAD_W), full),
            pl.BlockSpec((SEQ, HEAD_W), full),
            pl.BlockSpec((1, HEAD_W), full),
        ],
        out_specs=pl.BlockSpec((1, tq, HEAD_W), lambda b, h, i: (b, i, h)),
        out_shape=jax.ShapeDtypeStruct((BATCH, SEQ, ATTN_WIDTH), BF16),
        scratch_shapes=[pltpu.VMEM((SEQ + CTX_LEN, HEAD_W), BF16),
                        pltpu.VMEM((SEQ + CTX_LEN, 2 * HEAD_W), BF16)],
        compiler_params=_cparams(3, VMEM_LIMIT),
        name="attn",
    )(lam_params, z3, z3, z3, zc3, zc3, c_t, sa_t, sb_t, c_t, sa_t, sb_t, subln_w)


LAT_CHUNKS = SEQ // CHUNK
CTX_CHUNKS = CTX_LEN // CHUNK
PREP_ROWS = 512
INTRA_GROUP = 2
BOUNDED_GROUP = 16
DIAG_LOG2_BOUND = 96.0
SCAN_UNROLL = 32


def _tri(rev):
    r = lax.broadcasted_iota(jnp.int32, (CHUNK, CHUNK), 0)
    c = lax.broadcasted_iota(jnp.int32, (CHUNK, CHUNK), 1)
    return jnp.where((c >= r) if rev else (c <= r), 1.0, 0.0).astype(F32)


def _gate(f_raw, lower):
    f = lower + (1.0 - lower) * _sigmoid(f_raw)
    return 1.0 - f, jnp.log2(f)


def _prep(f_raw, lower, rev):
    n = f_raw.shape[0] // CHUNK
    k, logf = _gate(f_raw, lower)
    wide = jnp.concatenate([logf[i * CHUNK:(i + 1) * CHUNK] for i in range(n)], axis=1)
    bw = jnp.dot(_tri(rev), wide, precision=lax.Precision.HIGHEST, preferred_element_type=F32)
    t = 0 if rev else CHUNK - 1
    b = jnp.concatenate([bw[:, i * HEAD_W:(i + 1) * HEAD_W] for i in range(n)], axis=0)
    tots = [bw[t:t + 1, i * HEAD_W:(i + 1) * HEAD_W] for i in range(n)]
    b_tot = jnp.concatenate([jnp.broadcast_to(r, (CHUNK, HEAD_W)) for r in tots], axis=0)
    return k, b, b_tot, tots


def _diag_block(q_s, k_d, b_d, v_ref, base, rev):
    rows = lax.broadcasted_iota(jnp.int32, (8, HEAD_W), 0)
    halves = range(SUB // 8)
    qh = [q_s[pl.ds(base + 8 * u, 8), :] for u in halves]
    bh = [b_d[pl.ds(base + 8 * u, 8), :] for u in halves]
    acc = [jnp.zeros((8, HEAD_W), F32) for _ in halves]
    for s in range(SUB):
        bs = b_d[pl.ds(base + s, 1), :]
        ks = k_d[pl.ds(base + s, 1), :]
        vs = v_ref[0, pl.ds(base + s, 1), :]
        for u in halves:
            t_lo, t_hi = 8 * u, 8 * u + 7
            if (t_lo > s) if rev else (t_hi < s):
                continue
            w = jnp.exp2(bh[u] - bs)
            if (t_hi > s) if rev else (t_lo < s):
                keep = (rows + t_lo <= s) if rev else (rows + t_lo >= s)
                w = jnp.where(keep, w, 0.0)
            col = jnp.sum(qh[u] * w * ks, axis=-1, keepdims=True)
            acc[u] = acc[u] + col * vs
    return jnp.concatenate(acc, axis=0)


def _block_scores(q, k, b, rev, with_diag):
    blocks = []
    for i in range(CHUNK // SUB):
        lo, hi = SUB * i, SUB * (i + 1)
        if rev:
            e_lo, e_hi = (lo if with_diag else hi), CHUNK
            ref_row = b[hi:hi + 1] if hi < CHUNK else None
        else:
            e_lo, e_hi = 0, (hi if with_diag else lo)
            ref_row = b[lo - 1:lo] if lo > 0 else None
        if ref_row is None:
            if not with_diag:
                blocks.append(jnp.zeros((SUB, CHUNK), F32))
                continue
            ref_row = jnp.zeros((1, HEAD_W), F32)
        qt = (q[lo:hi] * jnp.exp2(b[lo:hi] - ref_row)).astype(BF16)
        kt = (k[e_lo:e_hi] * jnp.exp2(ref_row - b[e_lo:e_hi])).astype(BF16)
        parts = []
        if e_lo > 0:
            parts.append(jnp.zeros((e_lo, HEAD_W), BF16))
        parts.append(kt)
        if e_hi < CHUNK:
            parts.append(jnp.zeros((CHUNK - e_hi, HEAD_W), BF16))
        ktp = jnp.concatenate(parts, axis=0)
        blocks.append(lax.dot_general(qt, ktp, (((1,), (1,)), ((), ())), preferred_element_type=F32))
    return jnp.concatenate(blocks, axis=0)


def _hgrn2_kernel(lb_ref, ff_ref, fb_ref, v_ref, q_ref, ffc_ref, fbc_ref, vc_ref, o_ref,
                  q_s, k_s, b_s, qbar_s, kbar_s, dec_s, kbarc_s, decc_s, ob_s):
    lb = lb_ref[...]
    lower = []
    for d in range(2):
        l0, l1 = lb[d, 0:1], lb[d, 1:2]
        mx = jnp.maximum(l0, l1)
        e0, e1 = jnp.exp(l0 - mx), jnp.exp(l1 - mx)
        lower.append(e0 / (e0 + e1))
    f_lat = (ff_ref, fb_ref)
    f_ctx = (ffc_ref, fbc_ref)

    for d in range(2):
        k, b, b_tot, tots = _prep(f_ctx[d][0], lower[d], d == 1)
        kbarc_s[d] = (k * jnp.exp2(b_tot - b)).astype(BF16)
        for c in range(CTX_CHUNKS):
            decc_s[d, c] = jnp.broadcast_to(jnp.exp2(tots[c]), (8, HEAD_W))

    def prep_body(g, carry):
        rows = pl.ds(pl.multiple_of(g * PREP_ROWS, PREP_ROWS), PREP_ROWS)
        qr = q_ref[0, rows, :]
        q = qr * _sigmoid(qr)
        q_s[rows, :] = q
        for d in range(2):
            k, b, b_tot, tots = _prep(f_lat[d][0, rows, :], lower[d], d == 1)
            k_s[d, rows, :] = k
            b_s[d, rows, :] = b
            qbar_s[d, rows, :] = (q * jnp.exp2(b)).astype(BF16)
            kbar_s[d, rows, :] = (k * jnp.exp2(b_tot - b)).astype(BF16)
            for i in range(PREP_ROWS // CHUNK):
                dec_s[d, g * (PREP_ROWS // CHUNK) + i] = jnp.broadcast_to(jnp.exp2(tots[i]), (8, HEAD_W))
        return carry

    lax.fori_loop(0, SEQ // PREP_ROWS, prep_body, 0)

    def chunk_scores(base, with_diag):
        rows = pl.ds(base, CHUNK)
        out = []
        for d in range(2):
            a = _block_scores(q_s[rows, :], k_s[d, rows, :], b_s[d, rows, :], d == 1, with_diag)
            if with_diag:
                r = lax.broadcasted_iota(jnp.int32, a.shape, 0)
                c = lax.broadcasted_iota(jnp.int32, a.shape, 1)
                a = jnp.where((c >= r) if d == 1 else (c <= r), a, 0.0)
            out.append(a.astype(BF16))
        return out

    def value_products(bases, scores):
        pv = []
        for j, base in enumerate(bases):
            v16 = v_ref[0, pl.ds(base, CHUNK), :].astype(BF16)
            pv.append(jnp.dot(scores[j][0], v16, preferred_element_type=F32)
                      + jnp.dot(scores[j][1], v16, preferred_element_type=F32))
        return pv

    def intra_body(g, carry):
        bases = [pl.multiple_of((g * INTRA_GROUP + j) * CHUNK, CHUNK) for j in range(INTRA_GROUP)]

        def diag(base):
            blocks = []
            for i in range(CHUNK // SUB):
                blk = pl.multiple_of(base + SUB * i, SUB)
                blocks.append(_diag_block(q_s, k_s.at[0], b_s.at[0], v_ref, blk, False)
                              + _diag_block(q_s, k_s.at[1], b_s.at[1], v_ref, blk, True))
            return jnp.concatenate(blocks, axis=0)

        scores = [chunk_scores(base, False) for base in bases]
        od = [diag(bases[0])]
        pv = value_products(bases, scores)
        od += [diag(base) for base in bases[1:]]
        for j, base in enumerate(bases):
            rows = pl.ds(base, CHUNK)
            o_ref[0, rows, :] = (o_ref[0, rows, :] + ob_s[rows, :]) + (pv[j] + od[j])
        return carry

    def intra_body_bounded(g, carry):
        bases = [pl.multiple_of((g * BOUNDED_GROUP + j) * CHUNK, CHUNK) for j in range(BOUNDED_GROUP)]
        pv = value_products(bases, [chunk_scores(base, True) for base in bases])
        for j, base in enumerate(bases):
            rows = pl.ds(base, CHUNK)
            o_ref[0, rows, :] = (o_ref[0, rows, :] + ob_s[rows, :]) + pv[j]
        return carry

    def ctx_states():
        steps = [(d, (CTX_CHUNKS - 1 - c) if d == 1 else c) for c in range(CTX_CHUNKS) for d in range(2)]
        upd = [lax.dot_general(vc_ref[0, cc * CHUNK:(cc + 1) * CHUNK, :].astype(BF16),
                               kbarc_s[d, cc * CHUNK:(cc + 1) * CHUNK, :], (((0,), (0,)), ((), ())),
                               preferred_element_type=F32) for d, cc in steps]
        sts = [None, None]
        for n, (d, cc) in enumerate(steps):
            sts[d] = upd[n] if sts[d] is None else decc_s[d, cc][0:1] * sts[d] + upd[n]
        return tuple(sts)

    def lat_body(it, sts):
        sts = list(sts)
        steps = []
        for u in range(SCAN_UNROLL):
            c = it * SCAN_UNROLL + u
            for d in range(2):
                cc = (LAT_CHUNKS - 1 - c) if d == 1 else c
                steps.append((d, cc, pl.ds(pl.multiple_of(cc * CHUNK, CHUNK), CHUNK)))
        upd = [lax.dot_general(v_ref[0, rows, :].astype(BF16), kbar_s[d, rows, :], (((0,), (0,)), ((), ())),
                               preferred_element_type=F32) for d, cc, rows in steps]
        for n, (d, cc, rows) in enumerate(steps):
            inter = lax.dot_general(qbar_s[d, rows, :], sts[d].astype(BF16), (((1,), (1,)), ((), ())),
                                    preferred_element_type=F32)
            if d == 0:
                o_ref[0, rows, :] = inter
            else:
                ob_s[rows, :] = inter
            sts[d] = dec_s[d, cc][0:1] * sts[d] + upd[n]
        return tuple(sts)

    lax.fori_loop(0, LAT_CHUNKS // SCAN_UNROLL, lat_body, ctx_states())
    worst = jnp.min(jnp.minimum(lower[0], lower[1]))
    bounded = worst >= 2.0 ** (-DIAG_LOG2_BOUND / SUB)

    @pl.when(bounded)
    def _():
        lax.fori_loop(0, LAT_CHUNKS // BOUNDED_GROUP, intra_body_bounded, 0)

    @pl.when(jnp.logical_not(bounded))
    def _():
        lax.fori_loop(0, LAT_CHUNKS // INTRA_GROUP, intra_body, 0)


def _hgrn2_call(z3, zc3, rec_lb):
    lat = lambda col: pl.BlockSpec((1, SEQ, HEAD_W), lambda b, h: (b, 0, col + h))
    ctx = lambda col: pl.BlockSpec((1, CTX_LEN, HEAD_W), lambda b, h: (b, 0, col + h))
    return pl.pallas_call(
        _hgrn2_kernel,
        grid=(BATCH, HEADS),
        in_specs=[
            pl.BlockSpec((2, 2, HEAD_W), lambda b, h: (0, 0, h)),
            lat(COL_RFF), lat(COL_RFB), lat(COL_RI), lat(COL_RQ),
            ctx(COL_RFF), ctx(COL_RFB), ctx(COL_RI),
        ],
        out_specs=pl.BlockSpec((1, SEQ, HEAD_W), lambda b, h: (b, 0, h)),
        out_shape=jax.ShapeDtypeStruct((BATCH, SEQ, REC_WIDTH), F32),
        scratch_shapes=[
            pltpu.VMEM((SEQ, HEAD_W), F32),
            pltpu.VMEM((2, SEQ, HEAD_W), F32),
            pltpu.VMEM((2, SEQ, HEAD_W), F32),
            pltpu.VMEM((2, SEQ, HEAD_W), BF16),
            pltpu.VMEM((2, SEQ, HEAD_W), BF16),
            pltpu.VMEM((2, LAT_CHUNKS, 8, HEAD_W), F32),
            pltpu.VMEM((2, CTX_LEN, HEAD_W), BF16),
            pltpu.VMEM((2, CTX_CHUNKS, 8, HEAD_W), F32),
            pltpu.VMEM((SEQ, HEAD_W), F32),
        ],
        compiler_params=_cparams(2, VMEM_LIMIT),
        name="hgrn2",
    )(rec_lb, z3, z3, z3, z3, zc3, zc3, zc3)


def _merge_kernel(att_ref, orec_ref, rg_ref, ga_ref, gr_ref, x_ref, g1_ref, sh2_ref, sc2_ref,
                  gnw_ref, n2w_ref, wba_ref, wbr_ref, wout_ref, x1_ref, h2_ref):
    rg = rg_ref[...]
    rec = _rms(orec_ref[...], gnw_ref[...]) * (rg * _sigmoid(rg))
    ya = jnp.dot(att_ref[...], wba_ref[...], preferred_element_type=F32)
    yr = jnp.dot(rec.astype(BF16), wbr_ref[...], preferred_element_type=F32)
    y = _sigmoid(ga_ref[...]) * ya + _sigmoid(gr_ref[...]) * yr
    x1 = x_ref[...] + g1_ref[0] * jnp.dot(y.astype(BF16), wout_ref[...], preferred_element_type=F32)
    x1_ref[...] = x1
    h2 = _rms(x1, n2w_ref[...]) * (1.0 + sc2_ref[0]) + sh2_ref[0]
    h2_ref[...] = h2.astype(BF16)


def _merge_call(att2d, orec2d, z2d, x2d, mod3, gnorm_w, norm2_w, wba, wbr, wout):
    tm = 256
    m = x2d.shape[0]
    row = lambda i: (i, 0)
    const = lambda i: (0, 0)
    modspec = lambda k: pl.BlockSpec((1, 1, D_MODEL), lambda i: ((i * tm) // SEQ, 0, k))
    resident = lambda shape: pl.BlockSpec(shape, const, pipeline_mode=pl.Buffered(1))
    return pl.pallas_call(
        _merge_kernel,
        grid=(m // tm,),
        in_specs=[
            pl.BlockSpec((tm, ATTN_WIDTH), row),
            pl.BlockSpec((tm, REC_WIDTH), row),
            pl.BlockSpec((tm, REC_WIDTH), lambda i: (i, 7)),
            pl.BlockSpec((tm, D_MODEL), lambda i: (i, 4)),
            pl.BlockSpec((tm, D_MODEL), lambda i: (i, 5)),
            pl.BlockSpec((tm, D_MODEL), row),
            modspec(2), modspec(3), modspec(4),
            pl.BlockSpec((1, REC_WIDTH), const),
            pl.BlockSpec((1, D_MODEL), const),
            resident((ATTN_WIDTH, D_MODEL)),
            resident((REC_WIDTH, D_MODEL)),
            resident((D_MODEL, D_MODEL)),
        ],
        out_specs=[pl.BlockSpec((tm, D_MODEL), row), pl.BlockSpec((tm, D_MODEL), row)],
        out_shape=[jax.ShapeDtypeStruct((m, D_MODEL), F32), jax.ShapeDtypeStruct((m, D_MODEL), BF16)],
        compiler_params=_cparams(1, VMEM_LIMIT),
        name="merge",
    )(att2d, orec2d, z2d, z2d, z2d, x2d, mod3, mod3, mod3, gnorm_w, norm2_w, wba, wbr, wout)


FFN_COLS = 256
FFN_PAD = 8


def _ffn_up_kernel(h_ref, wa_ref, wb_ref, cwa_ref, cwb_ref, cba_ref, cbb_ref, g_ref, u_s):
    h = h_ref[0]
    n_units = wa_ref.shape[1] // FFN_COLS
    units = []
    for c in range(n_units):
        cols = slice(c * FFN_COLS, (c + 1) * FFN_COLS)
        units.append((wa_ref, cwa_ref, cba_ref, cols))
        units.append((wb_ref, cwb_ref, cbb_ref, cols))

    pad = jnp.zeros((FFN_PAD, FFN_COLS), F32)
    for slot in range(2):
        u_s[slot, 0:FFN_PAD, :] = pad
        u_s[slot, FFN_PAD + SEQ:2 * FFN_PAD + SEQ, :] = pad

    def proj(n):
        w_ref, _, _, cols = units[n]
        u_s[n % 2, FFN_PAD:FFN_PAD + SEQ, :] = jnp.dot(h, w_ref[:, cols], preferred_element_type=F32)

    def conv(n):
        _, cw_ref, cb_ref, cols = units[n]
        cw = cw_ref[:, cols]
        buf = u_s.at[n % 2]
        return cb_ref[:, cols] + (buf[FFN_PAD - 1:FFN_PAD - 1 + SEQ, :] * cw[0:1]
                                  + buf[FFN_PAD:FFN_PAD + SEQ, :] * cw[1:2]
                                  + buf[FFN_PAD + 1:FFN_PAD + 1 + SEQ, :] * cw[2:3])

    proj(0)
    a = None
    for n in range(len(units)):
        if n + 1 < len(units):
            proj(n + 1)
        y = conv(n)
        if n % 2 == 0:
            a = y * _sigmoid(y)
        else:
            g_ref[0, :, units[n][3]] = (a * y).astype(BF16)


def _ffn_up_call(h3, w_up, conv_w, conv_b):
    tn = 512
    nb = FFN_DIM // tn
    return pl.pallas_call(
        _ffn_up_kernel,
        grid=(BATCH, nb),
        in_specs=[
            pl.BlockSpec((1, SEQ, D_MODEL), lambda b, j: (b, 0, 0)),
            pl.BlockSpec((D_MODEL, tn), lambda b, j: (0, j)),
            pl.BlockSpec((D_MODEL, tn), lambda b, j: (0, nb + j)),
            pl.BlockSpec((3, tn), lambda b, j: (0, j)),
            pl.BlockSpec((3, tn), lambda b, j: (0, nb + j)),
            pl.BlockSpec((1, tn), lambda b, j: (0, j)),
            pl.BlockSpec((1, tn), lambda b, j: (0, nb + j)),
        ],
        out_specs=pl.BlockSpec((1, SEQ, tn), lambda b, j: (b, 0, j)),
        out_shape=jax.ShapeDtypeStruct((BATCH, SEQ, FFN_DIM), BF16),
        scratch_shapes=[pltpu.VMEM((2, SEQ + 2 * FFN_PAD, FFN_COLS), F32)],
        compiler_params=_cparams(2, VMEM_LIMIT),
        name="ffn_up",
    )(h3, w_up, w_up, conv_w, conv_w, conv_b, conv_b)


def _ffn_down_kernel(g_ref, w_ref, x1_ref, g2_ref, fw_ref, o_ref):
    y = jnp.dot(g_ref[...], w_ref[...], preferred_element_type=F32)
    x2 = x1_ref[...] + g2_ref[0] * y
    o_ref[...] = _rms(x2, fw_ref[...])


def _ffn_down_call(g2d, w_down, x1, mod3, final_w):
    tm = 256
    m = g2d.shape[0]
    return pl.pallas_call(
        _ffn_down_kernel,
        grid=(m // tm,),
        in_specs=[
            pl.BlockSpec((tm, FFN_DIM), lambda i: (i, 0)),
            pl.BlockSpec((FFN_DIM, D_MODEL), lambda i: (0, 0), pipeline_mode=pl.Buffered(1)),
            pl.BlockSpec((tm, D_MODEL), lambda i: (i, 0)),
            pl.BlockSpec((1, 1, D_MODEL), lambda i: ((i * tm) // SEQ, 0, 5)),
            pl.BlockSpec((1, D_MODEL), lambda i: (0, 0)),
        ],
        out_specs=pl.BlockSpec((tm, D_MODEL), lambda i: (i, 0)),
        out_shape=jax.ShapeDtypeStruct((m, D_MODEL), F32),
        compiler_params=_cparams(1, VMEM_LIMIT),
        name="ffn_down",
    )(g2d, w_down, x1, mod3, final_w)


def _rope_tables():
    rows = SEQ // GRID_W
    r, col = jnp.meshgrid(jnp.arange(rows), jnp.arange(GRID_W), indexing="ij")
    pos = jnp.stack([r.reshape(-1), col.reshape(-1)], axis=-1).astype(F32)
    nq = HEAD_DIM // 4
    inv = ROPE_BASE ** (-jnp.arange(nq, dtype=F32) / nq)
    ang = pos[:, :, None] * inv
    cos, sin = jnp.cos(ang), jnp.sin(ang)
    lane = jnp.arange(HEAD_W)
    axis = (lane % HEAD_DIM) // (2 * nq)
    second = ((lane % (2 * nq)) // nq) == 1
    freq = lane % nq
    c_t = cos[:, axis, freq]
    s_t = sin[:, axis, freq]
    sa_t = jnp.where(second[None, :], 0.0, -s_t)
    sb_t = jnp.where(second[None, :], s_t, 0.0)
    return c_t, sa_t, sb_t


def kernel(x, c, ctx, c_ctx, w_mod, b_mod, norm1_w, w_in, lam_q1, lam_k1, lam_q2, lam_k2, subln_w,
           rec_lb, rec_gnorm_w, w_branch_attn, w_branch_rec, w_out, norm2_w, w_up, conv_w, conv_b,
           w_down, final_norm_w):
    m_lat = BATCH * SEQ
    c_all = jnp.concatenate([c, c_ctx[None, :], jnp.zeros((MOD_ROWS - BATCH - 1, D_MODEL), F32)], axis=0)
    mod = _mod_call(c_all, w_mod[0], b_mod[0][None, :])
    mod3 = mod.reshape(MOD_ROWS, 1, N_MOD * D_MODEL)

    w_in16 = w_in[0].astype(BF16)
    n1w = norm1_w[0][None, :]
    x2d = x.reshape(m_lat, D_MODEL)
    tm_lat = 1024
    z = _inproj_call(x2d, n1w, mod3, w_in16, IN_WIDTH, lambda i: (i * tm_lat) // SEQ, tm_lat, "inproj")
    zc = _inproj_call(ctx.reshape(BATCH * CTX_LEN, D_MODEL), n1w, mod3, w_in16, CTX_KV_WIDTH,
                      lambda i: CTX_MOD_ROW, 1024, "inproj_ctx")
    z3 = z.reshape(BATCH, SEQ, IN_WIDTH)
    zc3 = zc.reshape(BATCH, CTX_LEN, CTX_KV_WIDTH)

    lam_params = jnp.stack([lam_q1[0], lam_k1[0], lam_q2[0], lam_k2[0]], axis=0)
    att = _attn_call(z3, zc3, lam_params, _rope_tables(), subln_w[0][None, :])
    orec = _hgrn2_call(z3, zc3, rec_lb)

    x1, h2 = _merge_call(att.reshape(m_lat, ATTN_WIDTH), orec.reshape(m_lat, REC_WIDTH), z, x2d, mod3,
                         rec_gnorm_w[0][None, :], norm2_w[0][None, :],
                         w_branch_attn[0].astype(BF16), w_branch_rec[0].astype(BF16), w_out[0].astype(BF16))
    g = _ffn_up_call(h2.reshape(BATCH, SEQ, D_MODEL), w_up[0].astype(BF16), conv_w[0], conv_b[0][None, :])
    out = _ffn_down_call(g.reshape(m_lat, FFN_DIM), w_down[0].astype(BF16), x1, mod3, final_norm_w[None, :])
    return out.reshape(BATCH, SEQ, D_MODEL)
```

```python
import math

import jax
import jax.numpy as jnp
from jax import lax
from jax.experimental import pallas as pl
from jax.experimental.pallas import tpu as pltpu

F32 = jnp.float32
BF16 = jnp.bfloat16

D_MODEL = 2048
BATCH = 8
SEQ = 2048
GRID_W = 64
CTX_LEN = 256
EPS = 1e-6
N_MOD = 6
HEADS = 8
HEAD_DIM = 64
HEAD_W = 128
ATTN_WIDTH = 1024
REC_WIDTH = 1024
FFN_DIM = 5632
IN_WIDTH = 12288
CTX_KV_WIDTH = 5120
ROPE_BASE = 10000.0
LAM_INIT = 0.8 - 0.6 * math.exp(-0.3 * 0)

COL_AK, COL_AV, COL_RFF, COL_RFB, COL_RI, COL_AQ, COL_RQ = 0, 8, 16, 24, 32, 40, 48
MOD_ROWS = 16
CTX_MOD_ROW = BATCH

CHUNK = 64
SUB = 16
VMEM_LIMIT = 56 * 1024 * 1024
LOG2E = 1.4426950408889634


def _cparams(n_axes, vmem=None):
    return pltpu.CompilerParams(
        dimension_semantics=("arbitrary",) * n_axes,
        vmem_limit_bytes=vmem,
    )


def _sigmoid(x):
    return 1.0 / (1.0 + jnp.exp2(x * (-LOG2E)))


def _rms(x, w):
    return x * lax.rsqrt(jnp.mean(x * x, axis=-1, keepdims=True) + EPS) * w


def _mod_kernel(c_ref, w_ref, b_ref, o_ref):
    c = c_ref[...]
    a = (c * _sigmoid(c)).astype(BF16)
    o_ref[...] = jnp.dot(a, w_ref[...].astype(BF16), preferred_element_type=F32) + b_ref[...]


def _mod_call(c_all, w_mod, b_mod):
    tn = 1024
    n = w_mod.shape[1]
    return pl.pallas_call(
        _mod_kernel,
        grid=(n // tn,),
        in_specs=[
            pl.BlockSpec((MOD_ROWS, D_MODEL), lambda j: (0, 0)),
            pl.BlockSpec((D_MODEL, tn), lambda j: (0, j)),
            pl.BlockSpec((1, tn), lambda j: (0, j)),
        ],
        out_specs=pl.BlockSpec((MOD_ROWS, tn), lambda j: (0, j)),
        out_shape=jax.ShapeDtypeStruct((MOD_ROWS, n), F32),
        compiler_params=_cparams(1, VMEM_LIMIT),
        name="mod",
    )(c_all, w_mod, b_mod)


def _inproj_kernel(x_ref, nw_ref, sh_ref, sc_ref, w_ref, o_ref, h_ref):
    @pl.when(pl.program_id(1) == 0)
    def _():
        h = _rms(x_ref[...], nw_ref[...]) * (1.0 + sc_ref[0]) + sh_ref[0]
        h_ref[...] = h.astype(BF16)

    o_ref[...] = jnp.dot(h_ref[...], w_ref[...], preferred_element_type=F32)


def _inproj_call(x2d, norm_w, mod3, w_bf16, n_out, mod_row_of_tile, tm, name):
    m = x2d.shape[0]
    tn = 1024
    return pl.pallas_call(
        _inproj_kernel,
        grid=(m // tm, n_out // tn),
        in_specs=[
            pl.BlockSpec((tm, D_MODEL), lambda i, j: (i, 0)),
            pl.BlockSpec((1, D_MODEL), lambda i, j: (0, 0)),
            pl.BlockSpec((1, 1, D_MODEL), lambda i, j: (mod_row_of_tile(i), 0, 0)),
            pl.BlockSpec((1, 1, D_MODEL), lambda i, j: (mod_row_of_tile(i), 0, 1)),
            pl.BlockSpec((D_MODEL, tn), lambda i, j: (0, j)),
        ],
        out_specs=pl.BlockSpec((tm, tn), lambda i, j: (i, j)),
        out_shape=jax.ShapeDtypeStruct((m, n_out), F32),
        scratch_shapes=[pltpu.VMEM((tm, D_MODEL), BF16)],
        compiler_params=_cparams(2, VMEM_LIMIT),
        name=name,
    )(x2d, norm_w, mod3, mod3, w_bf16)


def _rope(x, c, sa, sb):
    return x * c + pltpu.roll(x, HEAD_W - 16, 1) * sa + pltpu.roll(x, 16, 1) * sb


def _attn_kernel(lamp_ref, q_ref, k_ref, v_ref, kc_ref, vc_ref, cq_ref, saq_ref, sbq_ref,
                 ck_ref, sak_ref, sbk_ref, subln_ref, o_ref, k_s, v_s):
    tq = q_ref.shape[1]
    n_sub = tq // ATTN_SUB

    @pl.when(pl.program_id(2) == 0)
    def _():
        k = _rope(k_ref[0], ck_ref[...], sak_ref[...], sbk_ref[...])
        k_s[0:SEQ, :] = k.astype(BF16)
        k_s[SEQ:SEQ + CTX_LEN, :] = kc_ref[0].astype(BF16)
        v_s[0:SEQ, 0:HEAD_W] = v_ref[0].astype(BF16)
        v_s[SEQ:SEQ + CTX_LEN, 0:HEAD_W] = vc_ref[0].astype(BF16)
        v_s[:, HEAD_W:2 * HEAD_W] = jnp.ones((SEQ + CTX_LEN, HEAD_W), BF16)

    lp = lamp_ref[...]
    lam = (jnp.exp(jnp.sum(lp[0:1] * lp[1:2], axis=-1, keepdims=True))
           - jnp.exp(jnp.sum(lp[2:3] * lp[3:4], axis=-1, keepdims=True)) + LAM_INIT)

    q = _rope(q_ref[0], cq_ref[...], saq_ref[...], sbq_ref[...]) * (HEAD_DIM ** -0.5 * LOG2E)
    lane = lax.broadcasted_iota(jnp.int32, q.shape, 1)
    q0 = jnp.where(lane < HEAD_DIM, q, 0.0).astype(BF16)
    q1 = jnp.where(lane >= HEAD_DIM, q, 0.0).astype(BF16)

    def scores(i):
        lo, hi = i * ATTN_SUB, (i + 1) * ATTN_SUB
        qq = jnp.concatenate([q0[lo:hi], q1[lo:hi]], axis=0)
        return lax.dot_general(qq, k_s[...], (((1,), (1,)), ((), ())), preferred_element_type=F32)

    s_next = scores(0)
    for i in range(n_sub):
        s = s_next
        if i + 1 < n_sub:
            s_next = scores(i + 1)
        e = jnp.exp2(s - jnp.max(s, axis=-1, keepdims=True)).astype(BF16)
        r = jnp.dot(e, v_s[...], preferred_element_type=F32)
        p = r[:, 0:HEAD_W] / r[:, HEAD_W:2 * HEAD_W]
        o = p[0:ATTN_SUB] - lam * p[ATTN_SUB:2 * ATTN_SUB]
        o = _rms(o, subln_ref[...]) * (1.0 - LAM_INIT)
        o_ref[0, i * ATTN_SUB:(i + 1) * ATTN_SUB, :] = o.astype(BF16)


ATTN_SUB = 256


def _attn_call(z3, zc3, lam_params, rope_tabs, subln_w):
    tq = 2048
    c_t, sa_t, sb_t = rope_tabs
    qmap = lambda b, h, i: (i, 0)
    full = lambda b, h, i: (0, 0)
    return pl.pallas_call(
        _attn_kernel,
        grid=(BATCH, HEADS, SEQ // tq),
        in_specs=[
            pl.BlockSpec((4, HEAD_DIM), full),
            pl.BlockSpec((1, tq, HEAD_W), lambda b, h, i: (b, i, COL_AQ + h)),
            pl.BlockSpec((1, SEQ, HEAD_W), lambda b, h, i: (b, 0, COL_AK + h)),
            pl.BlockSpec((1, SEQ, HEAD_W), lambda b, h, i: (b, 0, COL_AV + h)),
            pl.BlockSpec((1, CTX_LEN, HEAD_W), lambda b, h, i: (b, 0, COL_AK + h)),
            pl.BlockSpec((1, CTX_LEN, HEAD_W), lambda b, h, i: (b, 0, COL_AV + h)),
            pl.BlockSpec((tq, HEAD_W), qmap),
            pl.BlockSpec((tq, HEAD_W), qmap),
            pl.BlockSpec((tq, HEAD_W), qmap),
            pl.BlockSpec((SEQ, HEAD_W), full),
            pl.BlockSpec((SEQ, HEAD_W), full),
            pl.BlockSpec((SEQ, HEAD_W), full),
            pl.BlockSpec((1, HEAD_W), full),
        ],
        out_specs=pl.BlockSpec((1, tq, HEAD_W), lambda b, h, i: (b, i, h)),
        out_shape=jax.ShapeDtypeStruct((BATCH, SEQ, ATTN_WIDTH), BF16),
        scratch_shapes=[pltpu.VMEM((SEQ + CTX_LEN, HEAD_W), BF16),
                        pltpu.VMEM((SEQ + CTX_LEN, 2 * HEAD_W), BF16)],
        compiler_params=_cparams(3, VMEM_LIMIT),
        name="attn",
    )(lam_params, z3, z3, z3, zc3, zc3, c_t, sa_t, sb_t, c_t, sa_t, sb_t, subln_w)


LAT_CHUNKS = SEQ // CHUNK
CTX_CHUNKS = CTX_LEN // CHUNK
PREP_ROWS = 1024
INTRA_GROUP = 2
BOUNDED_GROUP = 32
DIAG_LOG2_BOUND = 96.0
SCAN_UNROLL = 32


def _tri(rev):
    r = lax.broadcasted_iota(jnp.int32, (CHUNK, CHUNK), 0)
    c = lax.broadcasted_iota(jnp.int32, (CHUNK, CHUNK), 1)
    return jnp.where((c >= r) if rev else (c <= r), 1.0, 0.0).astype(F32)


def _gate(f_raw, lower):
    f = lower + (1.0 - lower) * _sigmoid(f_raw)
    return 1.0 - f, jnp.log2(f)


def _prep(f_raw, lower, rev):
    n = f_raw.shape[0] // CHUNK
    k, logf = _gate(f_raw, lower)
    wide = jnp.concatenate([logf[i * CHUNK:(i + 1) * CHUNK] for i in range(n)], axis=1)
    bw = jnp.dot(_tri(rev), wide, precision=lax.Precision.HIGHEST, preferred_element_type=F32)
    t = 0 if rev else CHUNK - 1
    b = jnp.concatenate([bw[:, i * HEAD_W:(i + 1) * HEAD_W] for i in range(n)], axis=0)
    tots = [bw[t:t + 1, i * HEAD_W:(i + 1) * HEAD_W] for i in range(n)]
    b_tot = jnp.concatenate([jnp.broadcast_to(r, (CHUNK, HEAD_W)) for r in tots], axis=0)
    return k, b, b_tot, tots


def _diag_block(q_s, k_d, b_d, v_ref, base, rev):
    rows = lax.broadcasted_iota(jnp.int32, (8, HEAD_W), 0)
    halves = range(SUB // 8)
    qh = [q_s[pl.ds(base + 8 * u, 8), :] for u in halves]
    bh = [b_d[pl.ds(base + 8 * u, 8), :] for u in halves]
    acc = [jnp.zeros((8, HEAD_W), F32) for _ in halves]
    for s in range(SUB):
        bs = b_d[pl.ds(base + s, 1), :]
        ks = k_d[pl.ds(base + s, 1), :]
        vs = v_ref[0, pl.ds(base + s, 1), :]
        for u in halves:
            t_lo, t_hi = 8 * u, 8 * u + 7
            if (t_lo > s) if rev else (t_hi < s):
                continue
            w = jnp.exp2(bh[u] - bs)
            if (t_hi > s) if rev else (t_lo < s):
                keep = (rows + t_lo <= s) if rev else (rows + t_lo >= s)
                w = jnp.where(keep, w, 0.0)
            col = jnp.sum(qh[u] * w * ks, axis=-1, keepdims=True)
            acc[u] = acc[u] + col * vs
    return jnp.concatenate(acc, axis=0)


def _block_scores(q, k, b, rev, with_diag):
    blocks = []
    for i in range(CHUNK // SUB):
        lo, hi = SUB * i, SUB * (i + 1)
        if rev:
            e_lo, e_hi = (lo if with_diag else hi), CHUNK
            ref_row = b[hi:hi + 1] if hi < CHUNK else None
        else:
            e_lo, e_hi = 0, (hi if with_diag else lo)
            ref_row = b[lo - 1:lo] if lo > 0 else None
        if ref_row is None:
            if not with_diag:
                blocks.append(jnp.zeros((SUB, CHUNK), F32))
                continue
            ref_row = jnp.zeros((1, HEAD_W), F32)
        qt = (q[lo:hi] * jnp.exp2(b[lo:hi] - ref_row)).astype(BF16)
        kt = (k[e_lo:e_hi] * jnp.exp2(ref_row - b[e_lo:e_hi])).astype(BF16)
        parts = []
        if e_lo > 0:
            parts.append(jnp.zeros((e_lo, HEAD_W), BF16))
        parts.append(kt)
        if e_hi < CHUNK:
            parts.append(jnp.zeros((CHUNK - e_hi, HEAD_W), BF16))
        ktp = jnp.concatenate(parts, axis=0)
        blocks.append(lax.dot_general(qt, ktp, (((1,), (1,)), ((), ())), preferred_element_type=F32))
    return jnp.concatenate(blocks, axis=0)


def _hgrn2_kernel(lb_ref, ff_ref, fb_ref, v_ref, q_ref, ffc_ref, fbc_ref, vc_ref, *refs):
    cast_in, (o_ref, *cast_out) = refs[:N_RIDERS], refs[N_RIDERS:2 * N_RIDERS + 1]
    q_s, k_s, b_s, qbar_s, kbar_s, dec_s, kbarc_s, decc_s, ob_s = refs[2 * N_RIDERS + 1:]
    for w_f32, w_bf16 in zip(cast_in, cast_out):
        w_bf16[...] = w_f32[...].astype(BF16)

    lb = lb_ref[...]
    lower = []
    for d in range(2):
        l0, l1 = lb[d, 0:1], lb[d, 1:2]
        mx = jnp.maximum(l0, l1)
        e0, e1 = jnp.exp(l0 - mx), jnp.exp(l1 - mx)
        lower.append(e0 / (e0 + e1))
    f_lat = (ff_ref, fb_ref)
    f_ctx = (ffc_ref, fbc_ref)

    for d in range(2):
        k, b, b_tot, tots = _prep(f_ctx[d][0], lower[d], d == 1)
        kbarc_s[d] = (k * jnp.exp2(b_tot - b)).astype(BF16)
        for c in range(CTX_CHUNKS):
            decc_s[d, c] = jnp.broadcast_to(jnp.exp2(tots[c]), (8, HEAD_W))

    def prep_body(g, carry):
        rows = pl.ds(pl.multiple_of(g * PREP_ROWS, PREP_ROWS), PREP_ROWS)
        qr = q_ref[0, rows, :]
        q = qr * _sigmoid(qr)
        q_s[rows, :] = q
        for d in range(2):
            k, b, b_tot, tots = _prep(f_lat[d][0, rows, :], lower[d], d == 1)
            k_s[d, rows, :] = k
            b_s[d, rows, :] = b
            qbar_s[d, rows, :] = (q * jnp.exp2(b)).astype(BF16)
            kbar_s[d, rows, :] = (k * jnp.exp2(b_tot - b)).astype(BF16)
            for i in range(PREP_ROWS // CHUNK):
                dec_s[d, g * (PREP_ROWS // CHUNK) + i] = jnp.broadcast_to(jnp.exp2(tots[i]), (8, HEAD_W))
        return carry

    lax.fori_loop(0, SEQ // PREP_ROWS, prep_body, 0)

    def chunk_scores(base, with_diag):
        rows = pl.ds(base, CHUNK)
        out = []
        for d in range(2):
            a = _block_scores(q_s[rows, :], k_s[d, rows, :], b_s[d, rows, :], d == 1, with_diag)
            if with_diag:
                r = lax.broadcasted_iota(jnp.int32, a.shape, 0)
                c = lax.broadcasted_iota(jnp.int32, a.shape, 1)
                a = jnp.where((c >= r) if d == 1 else (c <= r), a, 0.0)
            out.append(a.astype(BF16))
        return out

    def value_products(bases, scores):
        pv = []
        for j, base in enumerate(bases):
            v16 = v_ref[0, pl.ds(base, CHUNK), :].astype(BF16)
            pv.append(jnp.dot(scores[j][0], v16, preferred_element_type=F32)
                      + jnp.dot(scores[j][1], v16, preferred_element_type=F32))
        return pv

    def intra_body(g, carry):
        bases = [pl.multiple_of((g * INTRA_GROUP + j) * CHUNK, CHUNK) for j in range(INTRA_GROUP)]

        def diag(base):
            blocks = []
            for i in range(CHUNK // SUB):
                blk = pl.multiple_of(base + SUB * i, SUB)
                blocks.append(_diag_block(q_s, k_s.at[0], b_s.at[0], v_ref, blk, False)
                              + _diag_block(q_s, k_s.at[1], b_s.at[1], v_ref, blk, True))
            return jnp.concatenate(blocks, axis=0)

        scores = [chunk_scores(base, False) for base in bases]
        od = [diag(bases[0])]
        pv = value_products(bases, scores)
        od += [diag(base) for base in bases[1:]]
        for j, base in enumerate(bases):
            rows = pl.ds(base, CHUNK)
            o_ref[0, rows, :] = (o_ref[0, rows, :] + ob_s[rows, :]) + (pv[j] + od[j])
        return carry

    def intra_body_bounded(g, carry):
        bases = [pl.multiple_of((g * BOUNDED_GROUP + j) * CHUNK, CHUNK) for j in range(BOUNDED_GROUP)]
        pv = value_products(bases, [chunk_scores(base, True) for base in bases])
        for j, base in enumerate(bases):
            rows = pl.ds(base, CHUNK)
            o_ref[0, rows, :] = (o_ref[0, rows, :] + ob_s[rows, :]) + pv[j]
        return carry

    def ctx_states():
        steps = [(d, (CTX_CHUNKS - 1 - c) if d == 1 else c) for c in range(CTX_CHUNKS) for d in range(2)]
        upd = [lax.dot_general(vc_ref[0, cc * CHUNK:(cc + 1) * CHUNK, :].astype(BF16),
                               kbarc_s[d, cc * CHUNK:(cc + 1) * CHUNK, :], (((0,), (0,)), ((), ())),
                               preferred_element_type=F32) for d, cc in steps]
        sts = [None, None]
        for n, (d, cc) in enumerate(steps):
            sts[d] = upd[n] if sts[d] is None else decc_s[d, cc][0:1] * sts[d] + upd[n]
        return tuple(sts)

    def lat_body(it, sts):
        sts = list(sts)
        steps = []
        for u in range(SCAN_UNROLL):
            c = it * SCAN_UNROLL + u
            for d in range(2):
                cc = (LAT_CHUNKS - 1 - c) if d == 1 else c
                steps.append((d, cc, pl.ds(pl.multiple_of(cc * CHUNK, CHUNK), CHUNK)))
        upd = [lax.dot_general(v_ref[0, rows, :].astype(BF16), kbar_s[d, rows, :], (((0,), (0,)), ((), ())),
                               preferred_element_type=F32) for d, cc, rows in steps]
        for n, (d, cc, rows) in enumerate(steps):
            inter = lax.dot_general(qbar_s[d, rows, :], sts[d].astype(BF16), (((1,), (1,)), ((), ())),
                                    preferred_element_type=F32)
            if d == 0:
                o_ref[0, rows, :] = inter
            else:
                ob_s[rows, :] = inter
            sts[d] = dec_s[d, cc][0:1] * sts[d] + upd[n]
        return tuple(sts)

    lax.fori_loop(0, LAT_CHUNKS // SCAN_UNROLL, lat_body, ctx_states())
    worst = jnp.min(jnp.minimum(lower[0], lower[1]))
    bounded = worst >= 2.0 ** (-DIAG_LOG2_BOUND / SUB)

    @pl.when(bounded)
    def _():
        lax.fori_loop(0, LAT_CHUNKS // BOUNDED_GROUP, intra_body_bounded, 0)

    @pl.when(jnp.logical_not(bounded))
    def _():
        lax.fori_loop(0, LAT_CHUNKS // INTRA_GROUP, intra_body, 0)


N_RIDERS = 4


def _hgrn2_call(z3, zc3, rec_lb, riders):
    lat = lambda col: pl.BlockSpec((1, SEQ, HEAD_W), lambda b, h: (b, 0, col + h))
    ctx = lambda col: pl.BlockSpec((1, CTX_LEN, HEAD_W), lambda b, h: (b, 0, col + h))
    assert len(riders) == N_RIDERS
    slab = lambda w: pl.BlockSpec((w.shape[0] // (BATCH * HEADS), w.shape[1]), lambda b, h: (b * HEADS + h, 0))
    return pl.pallas_call(
        _hgrn2_kernel,
        grid=(BATCH, HEADS),
        in_specs=[
            pl.BlockSpec((2, 2, HEAD_W), lambda b, h: (0, 0, h)),
            lat(COL_RFF), lat(COL_RFB), lat(COL_RI), lat(COL_RQ),
            ctx(COL_RFF), ctx(COL_RFB), ctx(COL_RI),
        ] + [slab(w) for w in riders],
        out_specs=[pl.BlockSpec((1, SEQ, HEAD_W), lambda b, h: (b, 0, h))] + [slab(w) for w in riders],
        out_shape=[jax.ShapeDtypeStruct((BATCH, SEQ, REC_WIDTH), F32)]
        + [jax.ShapeDtypeStruct(w.shape, BF16) for w in riders],
        scratch_shapes=[
            pltpu.VMEM((SEQ, HEAD_W), F32),
            pltpu.VMEM((2, SEQ, HEAD_W), F32),
            pltpu.VMEM((2, SEQ, HEAD_W), F32),
            pltpu.VMEM((2, SEQ, HEAD_W), BF16),
            pltpu.VMEM((2, SEQ, HEAD_W), BF16),
            pltpu.VMEM((2, LAT_CHUNKS, 8, HEAD_W), F32),
            pltpu.VMEM((2, CTX_LEN, HEAD_W), BF16),
            pltpu.VMEM((2, CTX_CHUNKS, 8, HEAD_W), F32),
            pltpu.VMEM((SEQ, HEAD_W), F32),
        ],
        compiler_params=_cparams(2, VMEM_LIMIT),
        name="hgrn2",
    )(rec_lb, z3, z3, z3, z3, zc3, zc3, zc3, *riders)


def _merge_kernel(att_ref, orec_ref, rg_ref, ga_ref, gr_ref, x_ref, g1_ref, sh2_ref, sc2_ref,
                  gnw_ref, n2w_ref, wba_ref, wbr_ref, wout_ref, x1_ref, h2_ref):
    rg = rg_ref[...]
    rec = _rms(orec_ref[...], gnw_ref[...]) * (rg * _sigmoid(rg))
    ya = jnp.dot(att_ref[...], wba_ref[...], preferred_element_type=F32)
    yr = jnp.dot(rec.astype(BF16), wbr_ref[...], preferred_element_type=F32)
    y = _sigmoid(ga_ref[...]) * ya + _sigmoid(gr_ref[...]) * yr
    x1 = x_ref[...] + g1_ref[0] * jnp.dot(y.astype(BF16), wout_ref[...], preferred_element_type=F32)
    x1_ref[...] = x1
    h2 = _rms(x1, n2w_ref[...]) * (1.0 + sc2_ref[0]) + sh2_ref[0]
    h2_ref[...] = h2.astype(BF16)


def _merge_call(att2d, orec2d, z2d, x2d, mod3, gnorm_w, norm2_w, wba, wbr, wout):
    tm = 256
    m = x2d.shape[0]
    row = lambda i: (i, 0)
    const = lambda i: (0, 0)
    modspec = lambda k: pl.BlockSpec((1, 1, D_MODEL), lambda i: ((i * tm) // SEQ, 0, k))
    resident = lambda shape: pl.BlockSpec(shape, const, pipeline_mode=pl.Buffered(1))
    return pl.pallas_call(
        _merge_kernel,
        grid=(m // tm,),
        in_specs=[
            pl.BlockSpec((tm, ATTN_WIDTH), row),
            pl.BlockSpec((tm, REC_WIDTH), row),
            pl.BlockSpec((tm, REC_WIDTH), lambda i: (i, 7)),
            pl.BlockSpec((tm, D_MODEL), lambda i: (i, 4)),
            pl.BlockSpec((tm, D_MODEL), lambda i: (i, 5)),
            pl.BlockSpec((tm, D_MODEL), row),
            modspec(2), modspec(3), modspec(4),
            pl.BlockSpec((1, REC_WIDTH), const),
            pl.BlockSpec((1, D_MODEL), const),
            resident((ATTN_WIDTH, D_MODEL)),
            resident((REC_WIDTH, D_MODEL)),
            resident((D_MODEL, D_MODEL)),
        ],
        out_specs=[pl.BlockSpec((tm, D_MODEL), row), pl.BlockSpec((tm, D_MODEL), row)],
        out_shape=[jax.ShapeDtypeStruct((m, D_MODEL), F32), jax.ShapeDtypeStruct((m, D_MODEL), BF16)],
        compiler_params=_cparams(1, VMEM_LIMIT),
        name="merge",
    )(att2d, orec2d, z2d, z2d, z2d, x2d, mod3, mod3, mod3, gnorm_w, norm2_w, wba, wbr, wout)


FFN_COLS = 256
FFN_PAD = 8


def _ffn_up_kernel(h_ref, wa_ref, wb_ref, cwa_ref, cwb_ref, cba_ref, cbb_ref, wd_ref, g_ref, wd16_ref, u_s):
    wd16_ref[...] = wd_ref[...].astype(BF16)
    h = h_ref[0]
    n_units = wa_ref.shape[1] // FFN_COLS
    units = []
    for c in range(n_units):
        cols = slice(c * FFN_COLS, (c + 1) * FFN_COLS)
        units.append((wa_ref, cwa_ref, cba_ref, cols))
        units.append((wb_ref, cwb_ref, cbb_ref, cols))

    pad = jnp.zeros((FFN_PAD, FFN_COLS), F32)
    for slot in range(2):
        u_s[slot, 0:FFN_PAD, :] = pad
        u_s[slot, FFN_PAD + SEQ:2 * FFN_PAD + SEQ, :] = pad

    def proj(n):
        w_ref, _, _, cols = units[n]
        u_s[n % 2, FFN_PAD:FFN_PAD + SEQ, :] = jnp.dot(h, w_ref[:, cols], preferred_element_type=F32)

    def conv(n):
        _, cw_ref, cb_ref, cols = units[n]
        cw = cw_ref[:, cols]
        buf = u_s.at[n % 2]
        return cb_ref[:, cols] + (buf[FFN_PAD - 1:FFN_PAD - 1 + SEQ, :] * cw[0:1]
                                  + buf[FFN_PAD:FFN_PAD + SEQ, :] * cw[1:2]
                                  + buf[FFN_PAD + 1:FFN_PAD + 1 + SEQ, :] * cw[2:3])

    proj(0)
    a = None
    for n in range(len(units)):
        if n + 1 < len(units):
            proj(n + 1)
        y = conv(n)
        if n % 2 == 0:
            a = y * _sigmoid(y)
        else:
            g_ref[0, :, units[n][3]] = (a * y).astype(BF16)


def _ffn_up_call(h3, w_up, conv_w, conv_b, w_down):
    tn = 512
    nb = FFN_DIM // tn
    wd_slab = pl.BlockSpec((FFN_DIM // (BATCH * nb), D_MODEL), lambda b, j: (b * nb + j, 0))
    return pl.pallas_call(
        _ffn_up_kernel,
        grid=(BATCH, nb),
        in_specs=[
            pl.BlockSpec((1, SEQ, D_MODEL), lambda b, j: (b, 0, 0)),
            pl.BlockSpec((D_MODEL, tn), lambda b, j: (0, j)),
            pl.BlockSpec((D_MODEL, tn), lambda b, j: (0, nb + j)),
            pl.BlockSpec((3, tn), lambda b, j: (0, j)),
            pl.BlockSpec((3, tn), lambda b, j: (0, nb + j)),
            pl.BlockSpec((1, tn), lambda b, j: (0, j)),
            pl.BlockSpec((1, tn), lambda b, j: (0, nb + j)),
            wd_slab,
        ],
        out_specs=[pl.BlockSpec((1, SEQ, tn), lambda b, j: (b, 0, j)), wd_slab],
        out_shape=[jax.ShapeDtypeStruct((BATCH, SEQ, FFN_DIM), BF16),
                   jax.ShapeDtypeStruct(w_down.shape, BF16)],
        scratch_shapes=[pltpu.VMEM((2, SEQ + 2 * FFN_PAD, FFN_COLS), F32)],
        compiler_params=_cparams(2, VMEM_LIMIT),
        name="ffn_up",
    )(h3, w_up, w_up, conv_w, conv_w, conv_b, conv_b, w_down)


def _ffn_down_kernel(g_ref, w_ref, x1_ref, g2_ref, fw_ref, o_ref):
    y = jnp.dot(g_ref[...], w_ref[...], preferred_element_type=F32)
    x2 = x1_ref[...] + g2_ref[0] * y
    o_ref[...] = _rms(x2, fw_ref[...])


def _ffn_down_call(g2d, w_down, x1, mod3, final_w):
    tm = 256
    m = g2d.shape[0]
    return pl.pallas_call(
        _ffn_down_kernel,
        grid=(m // tm,),
        in_specs=[
            pl.BlockSpec((tm, FFN_DIM), lambda i: (i, 0)),
            pl.BlockSpec((FFN_DIM, D_MODEL), lambda i: (0, 0), pipeline_mode=pl.Buffered(1)),
            pl.BlockSpec((tm, D_MODEL), lambda i: (i, 0)),
            pl.BlockSpec((1, 1, D_MODEL), lambda i: ((i * tm) // SEQ, 0, 5)),
            pl.BlockSpec((1, D_MODEL), lambda i: (0, 0)),
        ],
        out_specs=pl.BlockSpec((tm, D_MODEL), lambda i: (i, 0)),
        out_shape=jax.ShapeDtypeStruct((m, D_MODEL), F32),
        compiler_params=_cparams(1, VMEM_LIMIT),
        name="ffn_down",
    )(g2d, w_down, x1, mod3, final_w)


def _rope_tables():
    rows = SEQ // GRID_W
    r, col = jnp.meshgrid(jnp.arange(rows), jnp.arange(GRID_W), indexing="ij")
    pos = jnp.stack([r.reshape(-1), col.reshape(-1)], axis=-1).astype(F32)
    nq = HEAD_DIM // 4
    inv = ROPE_BASE ** (-jnp.arange(nq, dtype=F32) / nq)
    ang = pos[:, :, None] * inv
    cos, sin = jnp.cos(ang), jnp.sin(ang)
    lane = jnp.arange(HEAD_W)
    axis = (lane % HEAD_DIM) // (2 * nq)
    second = ((lane % (2 * nq)) // nq) == 1
    freq = lane % nq
    c_t = cos[:, axis, freq]
    s_t = sin[:, axis, freq]
    sa_t = jnp.where(second[None, :], 0.0, -s_t)
    sb_t = jnp.where(second[None, :], s_t, 0.0)
    return c_t, sa_t, sb_t


def kernel(x, c, ctx, c_ctx, w_mod, b_mod, norm1_w, w_in, lam_q1, lam_k1, lam_q2, lam_k2, subln_w,
           rec_lb, rec_gnorm_w, w_branch_attn, w_branch_rec, w_out, norm2_w, w_up, conv_w, conv_b,
           w_down, final_norm_w):
    m_lat = BATCH * SEQ
    c_all = jnp.concatenate([c, c_ctx[None, :], jnp.zeros((MOD_ROWS - BATCH - 1, D_MODEL), F32)], axis=0)
    mod = _mod_call(c_all, w_mod[0], b_mod[0][None, :])
    mod3 = mod.reshape(MOD_ROWS, 1, N_MOD * D_MODEL)

    w_in16 = w_in[0].astype(BF16)
    n1w = norm1_w[0][None, :]
    x2d = x.reshape(m_lat, D_MODEL)
    tm_lat = 1024
    z = _inproj_call(x2d, n1w, mod3, w_in16, IN_WIDTH, lambda i: (i * tm_lat) // SEQ, tm_lat, "inproj")
    zc = _inproj_call(ctx.reshape(BATCH * CTX_LEN, D_MODEL), n1w, mod3, w_in16, CTX_KV_WIDTH,
                      lambda i: CTX_MOD_ROW, 1024, "inproj_ctx")
    z3 = z.reshape(BATCH, SEQ, IN_WIDTH)
    zc3 = zc.reshape(BATCH, CTX_LEN, CTX_KV_WIDTH)

    lam_params = jnp.stack([lam_q1[0], lam_k1[0], lam_q2[0], lam_k2[0]], axis=0)
    att = _attn_call(z3, zc3, lam_params, _rope_tables(), subln_w[0][None, :])
    orec, w_up16, w_out16, w_ba16, w_br16 = _hgrn2_call(
        z3, zc3, rec_lb, [w_up[0], w_out[0], w_branch_attn[0], w_branch_rec[0]])

    x1, h2 = _merge_call(att.reshape(m_lat, ATTN_WIDTH), orec.reshape(m_lat, REC_WIDTH), z, x2d, mod3,
                         rec_gnorm_w[0][None, :], norm2_w[0][None, :],
                         w_ba16, w_br16, w_out16)
    g, w_down16 = _ffn_up_call(h2.reshape(BATCH, SEQ, D_MODEL), w_up16, conv_w[0], conv_b[0][None, :], w_down[0])
    out = _ffn_down_call(g.reshape(m_lat, FFN_DIM), w_down16, x1, mod3, final_norm_w[None, :])
    return out.reshape(BATCH, SEQ, D_MODEL)
```

```python
import math

import jax
import jax.numpy as jnp
from jax import lax
from jax.experimental import pallas as pl
from jax.experimental.pallas import tpu as pltpu

F32 = jnp.float32
BF16 = jnp.bfloat16

D_MODEL = 2048
BATCH = 8
SEQ = 2048
GRID_W = 64
CTX_LEN = 256
EPS = 1e-6
N_MOD = 6
HEADS = 8
HEAD_DIM = 64
HEAD_W = 128
ATTN_WIDTH = 1024
REC_WIDTH = 1024
FFN_DIM = 5632
IN_WIDTH = 12288
CTX_KV_WIDTH = 5120
ROPE_BASE = 10000.0
LAM_INIT = 0.8 - 0.6 * math.exp(-0.3 * 0)

COL_AK, COL_AV, COL_RFF, COL_RFB, COL_RI, COL_AQ, COL_RQ = 0, 8, 16, 24, 32, 40, 48
MOD_ROWS = 16
CTX_MOD_ROW = BATCH

CHUNK = 64
SUB = 16
VMEM_LIMIT = 56 * 1024 * 1024
LOG2E = 1.4426950408889634


def _cparams(n_axes, vmem=None):
    return pltpu.CompilerParams(
        dimension_semantics=("arbitrary",) * n_axes,
        vmem_limit_bytes=vmem,
    )


def _sigmoid(x):
    return 1.0 / (1.0 + jnp.exp2(x * (-LOG2E)))


def _rms(x, w):
    return x * lax.rsqrt(jnp.mean(x * x, axis=-1, keepdims=True) + EPS) * w


def _mod_kernel(c_ref, w_ref, b_ref, o_ref):
    c = c_ref[...]
    a = (c * _sigmoid(c)).astype(BF16)
    o_ref[...] = jnp.dot(a, w_ref[...].astype(BF16), preferred_element_type=F32) + b_ref[...]


def _mod_call(c_all, w_mod, b_mod):
    tn = 1024
    n = w_mod.shape[1]
    return pl.pallas_call(
        _mod_kernel,
        grid=(n // tn,),
        in_specs=[
            pl.BlockSpec((MOD_ROWS, D_MODEL), lambda j: (0, 0)),
            pl.BlockSpec((D_MODEL, tn), lambda j: (0, j)),
            pl.BlockSpec((1, tn), lambda j: (0, j)),
        ],
        out_specs=pl.BlockSpec((MOD_ROWS, tn), lambda j: (0, j)),
        out_shape=jax.ShapeDtypeStruct((MOD_ROWS, n), F32),
        compiler_params=_cparams(1, VMEM_LIMIT),
        name="mod",
    )(c_all, w_mod, b_mod)


def _inproj_kernel(x_ref, nw_ref, sh_ref, sc_ref, w_ref, o_ref, h_ref):
    @pl.when(pl.program_id(1) == 0)
    def _():
        h = _rms(x_ref[...], nw_ref[...]) * (1.0 + sc_ref[0]) + sh_ref[0]
        h_ref[...] = h.astype(BF16)

    o_ref[...] = jnp.dot(h_ref[...], w_ref[...].astype(BF16), preferred_element_type=F32)


def _inproj_call(x2d, norm_w, mod3, w_f32, n_out, mod_row_of_tile, tm, name):
    m = x2d.shape[0]
    tn = 1024
    return pl.pallas_call(
        _inproj_kernel,
        grid=(m // tm, n_out // tn),
        in_specs=[
            pl.BlockSpec((tm, D_MODEL), lambda i, j: (i, 0)),
            pl.BlockSpec((1, D_MODEL), lambda i, j: (0, 0)),
            pl.BlockSpec((1, 1, D_MODEL), lambda i, j: (mod_row_of_tile(i), 0, 0)),
            pl.BlockSpec((1, 1, D_MODEL), lambda i, j: (mod_row_of_tile(i), 0, 1)),
            pl.BlockSpec((D_MODEL, tn), lambda i, j: (0, j)),
        ],
        out_specs=pl.BlockSpec((tm, tn), lambda i, j: (i, j)),
        out_shape=jax.ShapeDtypeStruct((m, n_out), F32),
        scratch_shapes=[pltpu.VMEM((tm, D_MODEL), BF16)],
        compiler_params=_cparams(2, VMEM_LIMIT),
        name=name,
    )(x2d, norm_w, mod3, mod3, w_f32)


def _rope(x, c, sa, sb):
    return x * c + pltpu.roll(x, HEAD_W - 16, 1) * sa + pltpu.roll(x, 16, 1) * sb


def _attn_kernel(lamp_ref, q_ref, k_ref, v_ref, kc_ref, vc_ref, cq_ref, saq_ref, sbq_ref,
                 ck_ref, sak_ref, sbk_ref, subln_ref, o_ref, k_s, v_s):
    tq = q_ref.shape[1]
    n_sub = tq // ATTN_SUB

    @pl.when(pl.program_id(2) == 0)
    def _():
        k = _rope(k_ref[0], ck_ref[...], sak_ref[...], sbk_ref[...])
        k_s[0:SEQ, :] = k.astype(BF16)
        k_s[SEQ:SEQ + CTX_LEN, :] = kc_ref[0].astype(BF16)
        v_s[0:SEQ, 0:HEAD_W] = v_ref[0].astype(BF16)
        v_s[SEQ:SEQ + CTX_LEN, 0:HEAD_W] = vc_ref[0].astype(BF16)
        v_s[:, HEAD_W:2 * HEAD_W] = jnp.ones((SEQ + CTX_LEN, HEAD_W), BF16)

    lp = lamp_ref[...]
    lam = (jnp.exp(jnp.sum(lp[0:1] * lp[1:2], axis=-1, keepdims=True))
           - jnp.exp(jnp.sum(lp[2:3] * lp[3:4], axis=-1, keepdims=True)) + LAM_INIT)

    q = _rope(q_ref[0], cq_ref[...], saq_ref[...], sbq_ref[...]) * (HEAD_DIM ** -0.5 * LOG2E)
    lane = lax.broadcasted_iota(jnp.int32, q.shape, 1)
    q0 = jnp.where(lane < HEAD_DIM, q, 0.0).astype(BF16)
    q1 = jnp.where(lane >= HEAD_DIM, q, 0.0).astype(BF16)

    def scores(i):
        lo, hi = i * ATTN_SUB, (i + 1) * ATTN_SUB
        qq = jnp.concatenate([q0[lo:hi], q1[lo:hi]], axis=0)
        return lax.dot_general(qq, k_s[...], (((1,), (1,)), ((), ())), preferred_element_type=F32)

    s_next = scores(0)
    for i in range(n_sub):
        s = s_next
        if i + 1 < n_sub:
            s_next = scores(i + 1)
        e = jnp.exp2(s - jnp.max(s, axis=-1, keepdims=True)).astype(BF16)
        r = jnp.dot(e, v_s[...], preferred_element_type=F32)
        p = r[:, 0:HEAD_W] / r[:, HEAD_W:2 * HEAD_W]
        o = p[0:ATTN_SUB] - lam * p[ATTN_SUB:2 * ATTN_SUB]
        o = _rms(o, subln_ref[...]) * (1.0 - LAM_INIT)
        o_ref[0, i * ATTN_SUB:(i + 1) * ATTN_SUB, :] = o.astype(BF16)


ATTN_SUB = 256


def _attn_call(z3, zc3, lam_params, rope_tabs, subln_w):
    tq = 2048
    c_t, sa_t, sb_t = rope_tabs
    qmap = lambda b, h, i: (i, 0)
    full = lambda b, h, i: (0, 0)
    return pl.pallas_call(
        _attn_kernel,
        grid=(BATCH, HEADS, SEQ // tq),
        in_specs=[
            pl.BlockSpec((4, HEAD_DIM), full),
            pl.BlockSpec((1, tq, HEAD_W), lambda b, h, i: (b, i, COL_AQ + h)),
            pl.BlockSpec((1, SEQ, HEAD_W), lambda b, h, i: (b, 0, COL_AK + h)),
            pl.BlockSpec((1, SEQ, HEAD_W), lambda b, h, i: (b, 0, COL_AV + h)),
            pl.BlockSpec((1, CTX_LEN, HEAD_W), lambda b, h, i: (b, 0, COL_AK + h)),
            pl.BlockSpec((1, CTX_LEN, HEAD_W), lambda b, h, i: (b, 0, COL_AV + h)),
            pl.BlockSpec((tq, HEAD_W), qmap),
            pl.BlockSpec((tq, HEAD_W), qmap),
            pl.BlockSpec((tq, HEAD_W), qmap),
            pl.BlockSpec((SEQ, HEAD_W), full),
            pl.BlockSpec((SEQ, HEAD_W), full),
            pl.BlockSpec((SEQ, HEAD_W), full),
            pl.BlockSpec((1, HEAD_W), full),
        ],
        out_specs=pl.BlockSpec((1, tq, HEAD_W), lambda b, h, i: (b, i, h)),
        out_shape=jax.ShapeDtypeStruct((BATCH, SEQ, ATTN_WIDTH), BF16),
        scratch_shapes=[pltpu.VMEM((SEQ + CTX_LEN, HEAD_W), BF16),
                        pltpu.VMEM((SEQ + CTX_LEN, 2 * HEAD_W), BF16)],
        compiler_params=_cparams(3, VMEM_LIMIT),
        name="attn",
    )(lam_params, z3, z3, z3, zc3, zc3, c_t, sa_t, sb_t, c_t, sa_t, sb_t, subln_w)


LAT_CHUNKS = SEQ // CHUNK
CTX_CHUNKS = CTX_LEN // CHUNK
PREP_ROWS = 1024
INTRA_GROUP = 2
BOUNDED_GROUP = 32
DIAG_LOG2_BOUND = 96.0
SCAN_UNROLL = 32


def _tri(rev):
    r = lax.broadcasted_iota(jnp.int32, (CHUNK, CHUNK), 0)
    c = lax.broadcasted_iota(jnp.int32, (CHUNK, CHUNK), 1)
    return jnp.where((c >= r) if rev else (c <= r), 1.0, 0.0).astype(F32)


def _gate(f_raw, lower):
    f = lower + (1.0 - lower) * _sigmoid(f_raw)
    return 1.0 - f, jnp.log2(f)


def _prep(f_raw, lower, rev):
    n = f_raw.shape[0] // CHUNK
    k, logf = _gate(f_raw, lower)
    wide = jnp.concatenate([logf[i * CHUNK:(i + 1) * CHUNK] for i in range(n)], axis=1)
    bw = jnp.dot(_tri(rev), wide, precision=lax.Precision.HIGHEST, preferred_element_type=F32)
    t = 0 if rev else CHUNK - 1
    b = jnp.concatenate([bw[:, i * HEAD_W:(i + 1) * HEAD_W] for i in range(n)], axis=0)
    tots = [bw[t:t + 1, i * HEAD_W:(i + 1) * HEAD_W] for i in range(n)]
    b_tot = jnp.concatenate([jnp.broadcast_to(r, (CHUNK, HEAD_W)) for r in tots], axis=0)
    return k, b, b_tot, tots


def _diag_block(q_s, k_d, b_d, v_ref, base, rev):
    rows = lax.broadcasted_iota(jnp.int32, (8, HEAD_W), 0)
    halves = range(SUB // 8)
    qh = [q_s[pl.ds(base + 8 * u, 8), :] for u in halves]
    bh = [b_d[pl.ds(base + 8 * u, 8), :] for u in halves]
    acc = [jnp.zeros((8, HEAD_W), F32) for _ in halves]
    for s in range(SUB):
        bs = b_d[pl.ds(base + s, 1), :]
        ks = k_d[pl.ds(base + s, 1), :]
        vs = v_ref[0, pl.ds(base + s, 1), :]
        for u in halves:
            t_lo, t_hi = 8 * u, 8 * u + 7
            if (t_lo > s) if rev else (t_hi < s):
                continue
            w = jnp.exp2(bh[u] - bs)
            if (t_hi > s) if rev else (t_lo < s):
                keep = (rows + t_lo <= s) if rev else (rows + t_lo >= s)
                w = jnp.where(keep, w, 0.0)
            col = jnp.sum(qh[u] * w * ks, axis=-1, keepdims=True)
            acc[u] = acc[u] + col * vs
    return jnp.concatenate(acc, axis=0)


def _block_scores(q, k, b, rev, with_diag):
    blocks = []
    for i in range(CHUNK // SUB):
        lo, hi = SUB * i, SUB * (i + 1)
        if rev:
            e_lo, e_hi = (lo if with_diag else hi), CHUNK
            ref_row = b[hi:hi + 1] if hi < CHUNK else None
        else:
            e_lo, e_hi = 0, (hi if with_diag else lo)
            ref_row = b[lo - 1:lo] if lo > 0 else None
        if ref_row is None:
            if not with_diag:
                blocks.append(jnp.zeros((SUB, CHUNK), F32))
                continue
            ref_row = jnp.zeros((1, HEAD_W), F32)
        qt = (q[lo:hi] * jnp.exp2(b[lo:hi] - ref_row)).astype(BF16)
        kt = (k[e_lo:e_hi] * jnp.exp2(ref_row - b[e_lo:e_hi])).astype(BF16)
        parts = []
        if e_lo > 0:
            parts.append(jnp.zeros((e_lo, HEAD_W), BF16))
        parts.append(kt)
        if e_hi < CHUNK:
            parts.append(jnp.zeros((CHUNK - e_hi, HEAD_W), BF16))
        ktp = jnp.concatenate(parts, axis=0)
        blocks.append(lax.dot_general(qt, ktp, (((1,), (1,)), ((), ())), preferred_element_type=F32))
    return jnp.concatenate(blocks, axis=0)


def _hgrn2_kernel(lb_ref, ff_ref, fb_ref, v_ref, q_ref, ffc_ref, fbc_ref, vc_ref, *refs):
    cast_in, (o_ref, *cast_out) = refs[:N_RIDERS], refs[N_RIDERS:2 * N_RIDERS + 1]
    q_s, k_s, b_s, qbar_s, kbar_s, dec_s, kbarc_s, decc_s, ob_s = refs[2 * N_RIDERS + 1:]
    for w_f32, w_bf16 in zip(cast_in, cast_out):
        w_bf16[...] = w_f32[...].astype(BF16)

    lb = lb_ref[...]
    lower = []
    for d in range(2):
        l0, l1 = lb[d, 0:1], lb[d, 1:2]
        mx = jnp.maximum(l0, l1)
        e0, e1 = jnp.exp(l0 - mx), jnp.exp(l1 - mx)
        lower.append(e0 / (e0 + e1))
    f_lat = (ff_ref, fb_ref)
    f_ctx = (ffc_ref, fbc_ref)

    for d in range(2):
        k, b, b_tot, tots = _prep(f_ctx[d][0], lower[d], d == 1)
        kbarc_s[d] = (k * jnp.exp2(b_tot - b)).astype(BF16)
        for c in range(CTX_CHUNKS):
            decc_s[d, c] = jnp.broadcast_to(jnp.exp2(tots[c]), (8, HEAD_W))

    def prep_body(g, carry):
        rows = pl.ds(pl.multiple_of(g * PREP_ROWS, PREP_ROWS), PREP_ROWS)
        qr = q_ref[0, rows, :]
        q = qr * _sigmoid(qr)
        q_s[rows, :] = q
        for d in range(2):
            k, b, b_tot, tots = _prep(f_lat[d][0, rows, :], lower[d], d == 1)
            k_s[d, rows, :] = k
            b_s[d, rows, :] = b
            qbar_s[d, rows, :] = (q * jnp.exp2(b)).astype(BF16)
            kbar_s[d, rows, :] = (k * jnp.exp2(b_tot - b)).astype(BF16)
            for i in range(PREP_ROWS // CHUNK):
                dec_s[d, g * (PREP_ROWS // CHUNK) + i] = jnp.broadcast_to(jnp.exp2(tots[i]), (8, HEAD_W))
        return carry

    lax.fori_loop(0, SEQ // PREP_ROWS, prep_body, 0)

    def chunk_scores(base, with_diag):
        rows = pl.ds(base, CHUNK)
        out = []
        for d in range(2):
            a = _block_scores(q_s[rows, :], k_s[d, rows, :], b_s[d, rows, :], d == 1, with_diag)
            if with_diag:
                r = lax.broadcasted_iota(jnp.int32, a.shape, 0)
                c = lax.broadcasted_iota(jnp.int32, a.shape, 1)
                a = jnp.where((c >= r) if d == 1 else (c <= r), a, 0.0)
            out.append(a.astype(BF16))
        return out

    def value_products(bases, scores):
        pv = []
        for j, base in enumerate(bases):
            v16 = v_ref[0, pl.ds(base, CHUNK), :].astype(BF16)
            pv.append(jnp.dot(scores[j][0], v16, preferred_element_type=F32)
                      + jnp.dot(scores[j][1], v16, preferred_element_type=F32))
        return pv

    def intra_body(g, carry):
        bases = [pl.multiple_of((g * INTRA_GROUP + j) * CHUNK, CHUNK) for j in range(INTRA_GROUP)]

        def diag(base):
            blocks = []
            for i in range(CHUNK // SUB):
                blk = pl.multiple_of(base + SUB * i, SUB)
                blocks.append(_diag_block(q_s, k_s.at[0], b_s.at[0], v_ref, blk, False)
                              + _diag_block(q_s, k_s.at[1], b_s.at[1], v_ref, blk, True))
            return jnp.concatenate(blocks, axis=0)

        scores = [chunk_scores(base, False) for base in bases]
        od = [diag(bases[0])]
        pv = value_products(bases, scores)
        od += [diag(base) for base in bases[1:]]
        for j, base in enumerate(bases):
            rows = pl.ds(base, CHUNK)
            o_ref[0, rows, :] = (o_ref[0, rows, :] + ob_s[rows, :]) + (pv[j] + od[j])
        return carry

    def intra_body_bounded(g, carry):
        bases = [pl.multiple_of((g * BOUNDED_GROUP + j) * CHUNK, CHUNK) for j in range(BOUNDED_GROUP)]
        pv = value_products(bases, [chunk_scores(base, True) for base in bases])
        for j, base in enumerate(bases):
            rows = pl.ds(base, CHUNK)
            o_ref[0, rows, :] = (o_ref[0, rows, :] + ob_s[rows, :]) + pv[j]
        return carry

    def ctx_states():
        steps = [(d, (CTX_CHUNKS - 1 - c) if d == 1 else c) for c in range(CTX_CHUNKS) for d in range(2)]
        upd = [lax.dot_general(vc_ref[0, cc * CHUNK:(cc + 1) * CHUNK, :].astype(BF16),
                               kbarc_s[d, cc * CHUNK:(cc + 1) * CHUNK, :], (((0,), (0,)), ((), ())),
                               preferred_element_type=F32) for d, cc in steps]
        sts = [None, None]
        for n, (d, cc) in enumerate(steps):
            sts[d] = upd[n] if sts[d] is None else decc_s[d, cc][0:1] * sts[d] + upd[n]
        return tuple(sts)

    def lat_body(it, sts):
        sts = list(sts)
        steps = []
        for u in range(SCAN_UNROLL):
            c = it * SCAN_UNROLL + u
            for d in range(2):
                cc = (LAT_CHUNKS - 1 - c) if d == 1 else c
                steps.append((d, cc, pl.ds(pl.multiple_of(cc * CHUNK, CHUNK), CHUNK)))
        upd = [lax.dot_general(v_ref[0, rows, :].astype(BF16), kbar_s[d, rows, :], (((0,), (0,)), ((), ())),
                               preferred_element_type=F32) for d, cc, rows in steps]
        for n, (d, cc, rows) in enumerate(steps):
            inter = lax.dot_general(qbar_s[d, rows, :], sts[d].astype(BF16), (((1,), (1,)), ((), ())),
                                    preferred_element_type=F32)
            if d == 0:
                o_ref[0, rows, :] = inter
            else:
                ob_s[rows, :] = inter
            sts[d] = dec_s[d, cc][0:1] * sts[d] + upd[n]
        return tuple(sts)

    lax.fori_loop(0, LAT_CHUNKS // SCAN_UNROLL, lat_body, ctx_states())
    worst = jnp.min(jnp.minimum(lower[0], lower[1]))
    bounded = worst >= 2.0 ** (-DIAG_LOG2_BOUND / SUB)

    @pl.when(bounded)
    def _():
        lax.fori_loop(0, LAT_CHUNKS // BOUNDED_GROUP, intra_body_bounded, 0)

    @pl.when(jnp.logical_not(bounded))
    def _():
        lax.fori_loop(0, LAT_CHUNKS // INTRA_GROUP, intra_body, 0)


N_RIDERS = 4


def _hgrn2_call(z3, zc3, rec_lb, riders):
    lat = lambda col: pl.BlockSpec((1, SEQ, HEAD_W), lambda b, h: (b, 0, col + h))
    ctx = lambda col: pl.BlockSpec((1, CTX_LEN, HEAD_W), lambda b, h: (b, 0, col + h))
    assert len(riders) == N_RIDERS
    slab = lambda w: pl.BlockSpec((w.shape[0] // (BATCH * HEADS), w.shape[1]), lambda b, h: (b * HEADS + h, 0))
    return pl.pallas_call(
        _hgrn2_kernel,
        grid=(BATCH, HEADS),
        in_specs=[
            pl.BlockSpec((2, 2, HEAD_W), lambda b, h: (0, 0, h)),
            lat(COL_RFF), lat(COL_RFB), lat(COL_RI), lat(COL_RQ),
            ctx(COL_RFF), ctx(COL_RFB), ctx(COL_RI),
        ] + [slab(w) for w in riders],
        out_specs=[pl.BlockSpec((1, SEQ, HEAD_W), lambda b, h: (b, 0, h))] + [slab(w) for w in riders],
        out_shape=[jax.ShapeDtypeStruct((BATCH, SEQ, REC_WIDTH), F32)]
        + [jax.ShapeDtypeStruct(w.shape, BF16) for w in riders],
        scratch_shapes=[
            pltpu.VMEM((SEQ, HEAD_W), F32),
            pltpu.VMEM((2, SEQ, HEAD_W), F32),
            pltpu.VMEM((2, SEQ, HEAD_W), F32),
            pltpu.VMEM((2, SEQ, HEAD_W), BF16),
            pltpu.VMEM((2, SEQ, HEAD_W), BF16),
            pltpu.VMEM((2, LAT_CHUNKS, 8, HEAD_W), F32),
            pltpu.VMEM((2, CTX_LEN, HEAD_W), BF16),
            pltpu.VMEM((2, CTX_CHUNKS, 8, HEAD_W), F32),
            pltpu.VMEM((SEQ, HEAD_W), F32),
        ],
        compiler_params=_cparams(2, VMEM_LIMIT),
        name="hgrn2",
    )(rec_lb, z3, z3, z3, z3, zc3, zc3, zc3, *riders)


def _merge_kernel(att_ref, orec_ref, rg_ref, ga_ref, gr_ref, x_ref, g1_ref, sh2_ref, sc2_ref,
                  gnw_ref, n2w_ref, wba_ref, wbr_ref, wout_ref, x1_ref, h2_ref):
    rg = rg_ref[...]
    rec = _rms(orec_ref[...], gnw_ref[...]) * (rg * _sigmoid(rg))
    ya = jnp.dot(att_ref[...], wba_ref[...], preferred_element_type=F32)
    yr = jnp.dot(rec.astype(BF16), wbr_ref[...], preferred_element_type=F32)
    y = _sigmoid(ga_ref[...]) * ya + _sigmoid(gr_ref[...]) * yr
    x1 = x_ref[...] + g1_ref[0] * jnp.dot(y.astype(BF16), wout_ref[...], preferred_element_type=F32)
    x1_ref[...] = x1
    h2 = _rms(x1, n2w_ref[...]) * (1.0 + sc2_ref[0]) + sh2_ref[0]
    h2_ref[...] = h2.astype(BF16)


def _merge_call(att2d, orec2d, z2d, x2d, mod3, gnorm_w, norm2_w, wba, wbr, wout):
    tm = 256
    m = x2d.shape[0]
    row = lambda i: (i, 0)
    const = lambda i: (0, 0)
    modspec = lambda k: pl.BlockSpec((1, 1, D_MODEL), lambda i: ((i * tm) // SEQ, 0, k))
    resident = lambda shape: pl.BlockSpec(shape, const, pipeline_mode=pl.Buffered(1))
    return pl.pallas_call(
        _merge_kernel,
        grid=(m // tm,),
        in_specs=[
            pl.BlockSpec((tm, ATTN_WIDTH), row),
            pl.BlockSpec((tm, REC_WIDTH), row),
            pl.BlockSpec((tm, REC_WIDTH), lambda i: (i, 7)),
            pl.BlockSpec((tm, D_MODEL), lambda i: (i, 4)),
            pl.BlockSpec((tm, D_MODEL), lambda i: (i, 5)),
            pl.BlockSpec((tm, D_MODEL), row),
            modspec(2), modspec(3), modspec(4),
            pl.BlockSpec((1, REC_WIDTH), const),
            pl.BlockSpec((1, D_MODEL), const),
            resident((ATTN_WIDTH, D_MODEL)),
            resident((REC_WIDTH, D_MODEL)),
            resident((D_MODEL, D_MODEL)),
        ],
        out_specs=[pl.BlockSpec((tm, D_MODEL), row), pl.BlockSpec((tm, D_MODEL), row)],
        out_shape=[jax.ShapeDtypeStruct((m, D_MODEL), F32), jax.ShapeDtypeStruct((m, D_MODEL), BF16)],
        compiler_params=_cparams(1, VMEM_LIMIT),
        name="merge",
    )(att2d, orec2d, z2d, z2d, z2d, x2d, mod3, mod3, mod3, gnorm_w, norm2_w, wba, wbr, wout)


FFN_COLS = 256
FFN_PAD = 8


def _ffn_up_kernel(h_ref, wa_ref, wb_ref, cwa_ref, cwb_ref, cba_ref, cbb_ref, wd_ref, g_ref, wd16_ref, u_s):
    wd16_ref[...] = wd_ref[...].astype(BF16)
    h = h_ref[0]
    n_units = wa_ref.shape[1] // FFN_COLS
    units = []
    for c in range(n_units):
        cols = slice(c * FFN_COLS, (c + 1) * FFN_COLS)
        units.append((wa_ref, cwa_ref, cba_ref, cols))
        units.append((wb_ref, cwb_ref, cbb_ref, cols))

    pad = jnp.zeros((FFN_PAD, FFN_COLS), F32)
    for slot in range(2):
        u_s[slot, 0:FFN_PAD, :] = pad
        u_s[slot, FFN_PAD + SEQ:2 * FFN_PAD + SEQ, :] = pad

    def proj(n):
        w_ref, _, _, cols = units[n]
        u_s[n % 2, FFN_PAD:FFN_PAD + SEQ, :] = jnp.dot(h, w_ref[:, cols], preferred_element_type=F32)

    def conv(n):
        _, cw_ref, cb_ref, cols = units[n]
        cw = cw_ref[:, cols]
        buf = u_s.at[n % 2]
        return cb_ref[:, cols] + (buf[FFN_PAD - 1:FFN_PAD - 1 + SEQ, :] * cw[0:1]
                                  + buf[FFN_PAD:FFN_PAD + SEQ, :] * cw[1:2]
                                  + buf[FFN_PAD + 1:FFN_PAD + 1 + SEQ, :] * cw[2:3])

    proj(0)
    a = None
    for n in range(len(units)):
        if n + 1 < len(units):
            proj(n + 1)
        y = conv(n)
        if n % 2 == 0:
            a = y * _sigmoid(y)
        else:
            g_ref[0, :, units[n][3]] = (a * y).astype(BF16)


def _ffn_up_call(h3, w_up, conv_w, conv_b, w_down):
    tn = 512
    nb = FFN_DIM // tn
    wd_slab = pl.BlockSpec((FFN_DIM // (BATCH * nb), D_MODEL), lambda b, j: (b * nb + j, 0))
    return pl.pallas_call(
        _ffn_up_kernel,
        grid=(BATCH, nb),
        in_specs=[
            pl.BlockSpec((1, SEQ, D_MODEL), lambda b, j: (b, 0, 0)),
            pl.BlockSpec((D_MODEL, tn), lambda b, j: (0, j)),
            pl.BlockSpec((D_MODEL, tn), lambda b, j: (0, nb + j)),
            pl.BlockSpec((3, tn), lambda b, j: (0, j)),
            pl.BlockSpec((3, tn), lambda b, j: (0, nb + j)),
            pl.BlockSpec((1, tn), lambda b, j: (0, j)),
            pl.BlockSpec((1, tn), lambda b, j: (0, nb + j)),
            wd_slab,
        ],
        out_specs=[pl.BlockSpec((1, SEQ, tn), lambda b, j: (b, 0, j)), wd_slab],
        out_shape=[jax.ShapeDtypeStruct((BATCH, SEQ, FFN_DIM), BF16),
                   jax.ShapeDtypeStruct(w_down.shape, BF16)],
        scratch_shapes=[pltpu.VMEM((2, SEQ + 2 * FFN_PAD, FFN_COLS), F32)],
        compiler_params=_cparams(2, VMEM_LIMIT),
        name="ffn_up",
    )(h3, w_up, w_up, conv_w, conv_w, conv_b, conv_b, w_down)


def _ffn_down_kernel(g_ref, w_ref, x1_ref, g2_ref, fw_ref, o_ref):
    y = jnp.dot(g_ref[...], w_ref[...], preferred_element_type=F32)
    x2 = x1_ref[...] + g2_ref[0] * y
    o_ref[...] = _rms(x2, fw_ref[...])


def _ffn_down_call(g2d, w_down, x1, mod3, final_w):
    tm = 256
    m = g2d.shape[0]
    return pl.pallas_call(
        _ffn_down_kernel,
        grid=(m // tm,),
        in_specs=[
            pl.BlockSpec((tm, FFN_DIM), lambda i: (i, 0)),
            pl.BlockSpec((FFN_DIM, D_MODEL), lambda i: (0, 0), pipeline_mode=pl.Buffered(1)),
            pl.BlockSpec((tm, D_MODEL), lambda i: (i, 0)),
            pl.BlockSpec((1, 1, D_MODEL), lambda i: ((i * tm) // SEQ, 0, 5)),
            pl.BlockSpec((1, D_MODEL), lambda i: (0, 0)),
        ],
        out_specs=pl.BlockSpec((tm, D_MODEL), lambda i: (i, 0)),
        out_shape=jax.ShapeDtypeStruct((m, D_MODEL), F32),
        compiler_params=_cparams(1, VMEM_LIMIT),
        name="ffn_down",
    )(g2d, w_down, x1, mod3, final_w)


def _rope_tables():
    rows = SEQ // GRID_W
    r, col = jnp.meshgrid(jnp.arange(rows), jnp.arange(GRID_W), indexing="ij")
    pos = jnp.stack([r.reshape(-1), col.reshape(-1)], axis=-1).astype(F32)
    nq = HEAD_DIM // 4
    inv = ROPE_BASE ** (-jnp.arange(nq, dtype=F32) / nq)
    ang = pos[:, :, None] * inv
    cos, sin = jnp.cos(ang), jnp.sin(ang)
    lane = jnp.arange(HEAD_W)
    axis = (lane % HEAD_DIM) // (2 * nq)
    second = ((lane % (2 * nq)) // nq) == 1
    freq = lane % nq
    c_t = cos[:, axis, freq]
    s_t = sin[:, axis, freq]
    sa_t = jnp.where(second[None, :], 0.0, -s_t)
    sb_t = jnp.where(second[None, :], s_t, 0.0)
    return c_t, sa_t, sb_t


def kernel(x, c, ctx, c_ctx, w_mod, b_mod, norm1_w, w_in, lam_q1, lam_k1, lam_q2, lam_k2, subln_w,
           rec_lb, rec_gnorm_w, w_branch_attn, w_branch_rec, w_out, norm2_w, w_up, conv_w, conv_b,
           w_down, final_norm_w):
    m_lat = BATCH * SEQ
    c_all = jnp.concatenate([c, c_ctx[None, :], jnp.zeros((MOD_ROWS - BATCH - 1, D_MODEL), F32)], axis=0)
    mod = _mod_call(c_all, w_mod[0], b_mod[0][None, :])
    mod3 = mod.reshape(MOD_ROWS, 1, N_MOD * D_MODEL)

    n1w = norm1_w[0][None, :]
    x2d = x.reshape(m_lat, D_MODEL)
    tm_lat = 1024
    z = _inproj_call(x2d, n1w, mod3, w_in[0], IN_WIDTH, lambda i: (i * tm_lat) // SEQ, tm_lat, "inproj")
    zc = _inproj_call(ctx.reshape(BATCH * CTX_LEN, D_MODEL), n1w, mod3, w_in[0], CTX_KV_WIDTH,
                      lambda i: CTX_MOD_ROW, 1024, "inproj_ctx")
    z3 = z.reshape(BATCH, SEQ, IN_WIDTH)
    zc3 = zc.reshape(BATCH, CTX_LEN, CTX_KV_WIDTH)

    lam_params = jnp.stack([lam_q1[0], lam_k1[0], lam_q2[0], lam_k2[0]], axis=0)
    att = _attn_call(z3, zc3, lam_params, _rope_tables(), subln_w[0][None, :])
    orec, w_up16, w_out16, w_ba16, w_br16 = _hgrn2_call(
        z3, zc3, rec_lb, [w_up[0], w_out[0], w_branch_attn[0], w_branch_rec[0]])

    x1, h2 = _merge_call(att.reshape(m_lat, ATTN_WIDTH), orec.reshape(m_lat, REC_WIDTH), z, x2d, mod3,
                         rec_gnorm_w[0][None, :], norm2_w[0][None, :],
                         w_ba16, w_br16, w_out16)
    g, w_down16 = _ffn_up_call(h2.reshape(BATCH, SEQ, D_MODEL), w_up16, conv_w[0], conv_b[0][None, :], w_down[0])
    out = _ffn_down_call(g.reshape(m_lat, FFN_DIM), w_down16, x1, mod3, final_norm_w[None, :])
    return out.reshape(BATCH, SEQ, D_MODEL)
```

```python
import math

import jax
import jax.numpy as jnp
from jax import lax
from jax.experimental import pallas as pl
from jax.experimental.pallas import tpu as pltpu

F32 = jnp.float32
BF16 = jnp.bfloat16

D_MODEL = 2048
BATCH = 8
SEQ = 2048
GRID_W = 64
CTX_LEN = 256
EPS = 1e-6
N_MOD = 6
HEADS = 8
HEAD_DIM = 64
HEAD_W = 128
ATTN_WIDTH = 1024
REC_WIDTH = 1024
FFN_DIM = 5632
IN_WIDTH = 12288
CTX_KV_WIDTH = 5120
ROPE_BASE = 10000.0
LAM_INIT = 0.8 - 0.6 * math.exp(-0.3 * 0)

COL_AK, COL_AV, COL_RFF, COL_RFB, COL_RI, COL_AQ, COL_RQ = 0, 8, 16, 24, 32, 40, 48
MOD_ROWS = 16
CTX_MOD_ROW = BATCH

CHUNK = 64
SUB = 16
LOG2E = 1.4426950408889634

VMEM_LIMIT = 56 * 1024 * 1024
MOD_TN = 1024
INPROJ_TM, INPROJ_TN = 1024, 1024
ATTN_SUB = 256
MERGE_TM = 256
FFN_TN = 512
FFN_COLS = 256
FFN_DOWN_TM = 256


def _cparams(n_axes, vmem=None):
    return pltpu.CompilerParams(
        dimension_semantics=("arbitrary",) * n_axes,
        vmem_limit_bytes=vmem,
    )


def _sigmoid(x):
    return 1.0 / (1.0 + jnp.exp2(x * (-LOG2E)))


def _rms(x, w):
    return x * lax.rsqrt(jnp.mean(x * x, axis=-1, keepdims=True) + EPS) * w


def _mod_kernel(c_ref, w_ref, b_ref, o_ref):
    c = c_ref[...]
    a = (c * _sigmoid(c)).astype(BF16)
    o_ref[...] = jnp.dot(a, w_ref[...].astype(BF16), preferred_element_type=F32) + b_ref[...]


def _mod_call(c_all, w_mod, b_mod):
    tn = MOD_TN
    n = w_mod.shape[1]
    return pl.pallas_call(
        _mod_kernel,
        grid=(n // tn,),
        in_specs=[
            pl.BlockSpec((MOD_ROWS, D_MODEL), lambda j: (0, 0)),
            pl.BlockSpec((D_MODEL, tn), lambda j: (0, j)),
            pl.BlockSpec((1, tn), lambda j: (0, j)),
        ],
        out_specs=pl.BlockSpec((MOD_ROWS, tn), lambda j: (0, j)),
        out_shape=jax.ShapeDtypeStruct((MOD_ROWS, n), F32),
        compiler_params=_cparams(1, VMEM_LIMIT),
        name="mod",
    )(c_all, w_mod, b_mod)


def _inproj_kernel(x_ref, nw_ref, sh_ref, sc_ref, w_ref, o_ref, h_ref):
    @pl.when(pl.program_id(1) == 0)
    def _():
        h = _rms(x_ref[...], nw_ref[...]) * (1.0 + sc_ref[0]) + sh_ref[0]
        h_ref[...] = h.astype(BF16)

    o_ref[...] = jnp.dot(h_ref[...], w_ref[...], preferred_element_type=F32)


def _inproj_call(x2d, norm_w, mod3, w_bf16, n_out, mod_row_of_tile, tm, name):
    m = x2d.shape[0]
    tn = INPROJ_TN
    return pl.pallas_call(
        _inproj_kernel,
        grid=(m // tm, n_out // tn),
        in_specs=[
            pl.BlockSpec((tm, D_MODEL), lambda i, j: (i, 0)),
            pl.BlockSpec((1, D_MODEL), lambda i, j: (0, 0)),
            pl.BlockSpec((1, 1, D_MODEL), lambda i, j: (mod_row_of_tile(i), 0, 0)),
            pl.BlockSpec((1, 1, D_MODEL), lambda i, j: (mod_row_of_tile(i), 0, 1)),
            pl.BlockSpec((D_MODEL, tn), lambda i, j: (0, j)),
        ],
        out_specs=pl.BlockSpec((tm, tn), lambda i, j: (i, j)),
        out_shape=jax.ShapeDtypeStruct((m, n_out), F32),
        scratch_shapes=[pltpu.VMEM((tm, D_MODEL), BF16)],
        compiler_params=_cparams(2, VMEM_LIMIT),
        name=name,
    )(x2d, norm_w, mod3, mod3, w_bf16)


def _rope(x, c, sa, sb):
    return x * c + pltpu.roll(x, HEAD_W - 16, 1) * sa + pltpu.roll(x, 16, 1) * sb


def _attn_kernel(lamp_ref, q_ref, k_ref, v_ref, kc_ref, vc_ref, c_ref, sa_ref, sb_ref, subln_ref, o_ref, k_s, v_s):
    n_sub = SEQ // ATTN_SUB
    c, sa, sb = c_ref[...], sa_ref[...], sb_ref[...]

    k_s[0:SEQ, :] = _rope(k_ref[0], c, sa, sb).astype(BF16)
    k_s[SEQ:SEQ + CTX_LEN, :] = kc_ref[0].astype(BF16)
    v_s[0:SEQ, 0:HEAD_W] = v_ref[0].astype(BF16)
    v_s[SEQ:SEQ + CTX_LEN, 0:HEAD_W] = vc_ref[0].astype(BF16)
    v_s[:, HEAD_W:2 * HEAD_W] = jnp.ones((SEQ + CTX_LEN, HEAD_W), BF16)

    lp = lamp_ref[...]
    lam = (jnp.exp(jnp.sum(lp[0:1] * lp[1:2], axis=-1, keepdims=True))
           - jnp.exp(jnp.sum(lp[2:3] * lp[3:4], axis=-1, keepdims=True)) + LAM_INIT)

    q = _rope(q_ref[0], c, sa, sb) * (HEAD_DIM ** -0.5 * LOG2E)
    lane = lax.broadcasted_iota(jnp.int32, q.shape, 1)
    q0 = jnp.where(lane < HEAD_DIM, q, 0.0).astype(BF16)
    q1 = jnp.where(lane >= HEAD_DIM, q, 0.0).astype(BF16)

    def scores(i):
        lo, hi = i * ATTN_SUB, (i + 1) * ATTN_SUB
        qq = jnp.concatenate([q0[lo:hi], q1[lo:hi]], axis=0)
        return lax.dot_general(qq, k_s[...], (((1,), (1,)), ((), ())), preferred_element_type=F32)

    s_next = scores(0)
    for i in range(n_sub):
        s = s_next
        if i + 1 < n_sub:
            s_next = scores(i + 1)
        e = jnp.exp2(s - jnp.max(s, axis=-1, keepdims=True)).astype(BF16)
        r = jnp.dot(e, v_s[...], preferred_element_type=F32)
        p = r[:, 0:HEAD_W] / r[:, HEAD_W:2 * HEAD_W]
        o = p[0:ATTN_SUB] - lam * p[ATTN_SUB:2 * ATTN_SUB]
        o = _rms(o, subln_ref[...]) * (1.0 - LAM_INIT)
        o_ref[0, i * ATTN_SUB:(i + 1) * ATTN_SUB, :] = o.astype(BF16)


def _attn_call(z3, zc3, lam_params, rope_tabs, subln_w):
    lat = lambda col: pl.BlockSpec((1, SEQ, HEAD_W), lambda b, h: (b, 0, col + h))
    ctx = lambda col: pl.BlockSpec((1, CTX_LEN, HEAD_W), lambda b, h: (b, 0, col + h))
    full = lambda shape: pl.BlockSpec(shape, lambda b, h: (0, 0))
    return pl.pallas_call(
        _attn_kernel,
        grid=(BATCH, HEADS),
        in_specs=[full((4, HEAD_DIM)), lat(COL_AQ), lat(COL_AK), lat(COL_AV), ctx(COL_AK), ctx(COL_AV),
                  full((SEQ, HEAD_W)), full((SEQ, HEAD_W)), full((SEQ, HEAD_W)), full((1, HEAD_W))],
        out_specs=pl.BlockSpec((1, SEQ, HEAD_W), lambda b, h: (b, 0, h)),
        out_shape=jax.ShapeDtypeStruct((BATCH, SEQ, ATTN_WIDTH), BF16),
        scratch_shapes=[pltpu.VMEM((SEQ + CTX_LEN, HEAD_W), BF16),
                        pltpu.VMEM((SEQ + CTX_LEN, 2 * HEAD_W), BF16)],
        compiler_params=_cparams(2, VMEM_LIMIT),
        name="attn",
    )(lam_params, z3, z3, z3, zc3, zc3, *rope_tabs, subln_w)


LAT_CHUNKS = SEQ // CHUNK
CTX_CHUNKS = CTX_LEN // CHUNK
PREP_ROWS = 1024
INTRA_GROUP = 2
BOUNDED_GROUP = 32
DIAG_LOG2_BOUND = 96.0
SCAN_UNROLL = 32


def _tri(rev):
    r = lax.broadcasted_iota(jnp.int32, (CHUNK, CHUNK), 0)
    c = lax.broadcasted_iota(jnp.int32, (CHUNK, CHUNK), 1)
    return jnp.where((c >= r) if rev else (c <= r), 1.0, 0.0).astype(F32)


def _gate(f_raw, lower):
    f = lower + (1.0 - lower) * _sigmoid(f_raw)
    return 1.0 - f, jnp.log2(f)


def _prep(f_raw, lower, rev):
    n = f_raw.shape[0] // CHUNK
    k, logf = _gate(f_raw, lower)
    wide = jnp.concatenate([logf[i * CHUNK:(i + 1) * CHUNK] for i in range(n)], axis=1)
    bw = jnp.dot(_tri(rev), wide, precision=lax.Precision.HIGHEST, preferred_element_type=F32)
    t = 0 if rev else CHUNK - 1
    b = jnp.concatenate([bw[:, i * HEAD_W:(i + 1) * HEAD_W] for i in range(n)], axis=0)
    tots = [bw[t:t + 1, i * HEAD_W:(i + 1) * HEAD_W] for i in range(n)]
    b_tot = jnp.concatenate([jnp.broadcast_to(r, (CHUNK, HEAD_W)) for r in tots], axis=0)
    return k, b, b_tot, tots


def _diag_block(q_s, k_d, b_d, v_ref, base, rev):
    rows = lax.broadcasted_iota(jnp.int32, (8, HEAD_W), 0)
    halves = range(SUB // 8)
    qh = [q_s[pl.ds(base + 8 * u, 8), :] for u in halves]
    bh = [b_d[pl.ds(base + 8 * u, 8), :] for u in halves]
    acc = [jnp.zeros((8, HEAD_W), F32) for _ in halves]
    for s in range(SUB):
        bs = b_d[pl.ds(base + s, 1), :]
        ks = k_d[pl.ds(base + s, 1), :]
        vs = v_ref[0, pl.ds(base + s, 1), :]
        for u in halves:
            t_lo, t_hi = 8 * u, 8 * u + 7
            if (t_lo > s) if rev else (t_hi < s):
                continue
            w = jnp.exp2(bh[u] - bs)
            if (t_hi > s) if rev else (t_lo < s):
                keep = (rows + t_lo <= s) if rev else (rows + t_lo >= s)
                w = jnp.where(keep, w, 0.0)
            col = jnp.sum(qh[u] * w * ks, axis=-1, keepdims=True)
            acc[u] = acc[u] + col * vs
    return jnp.concatenate(acc, axis=0)


def _block_scores(q, k, b, rev, with_diag):
    blocks = []
    for i in range(CHUNK // SUB):
        lo, hi = SUB * i, SUB * (i + 1)
        if rev:
            e_lo, e_hi = (lo if with_diag else hi), CHUNK
            ref_row = b[hi:hi + 1] if hi < CHUNK else None
        else:
            e_lo, e_hi = 0, (hi if with_diag else lo)
            ref_row = b[lo - 1:lo] if lo > 0 else None
        if ref_row is None:
            if not with_diag:
                blocks.append(jnp.zeros((SUB, CHUNK), F32))
                continue
            ref_row = jnp.zeros((1, HEAD_W), F32)
        qt = (q[lo:hi] * jnp.exp2(b[lo:hi] - ref_row)).astype(BF16)
        kt = (k[e_lo:e_hi] * jnp.exp2(ref_row - b[e_lo:e_hi])).astype(BF16)
        parts = []
        if e_lo > 0:
            parts.append(jnp.zeros((e_lo, HEAD_W), BF16))
        parts.append(kt)
        if e_hi < CHUNK:
            parts.append(jnp.zeros((CHUNK - e_hi, HEAD_W), BF16))
        ktp = jnp.concatenate(parts, axis=0)
        blocks.append(lax.dot_general(qt, ktp, (((1,), (1,)), ((), ())), preferred_element_type=F32))
    return jnp.concatenate(blocks, axis=0)


N_RIDERS = 4


def _hgrn2_kernel(lb_ref, ff_ref, fb_ref, v_ref, q_ref, ffc_ref, fbc_ref, vc_ref, *refs):
    cast_in, (o_ref, *cast_out) = refs[:N_RIDERS], refs[N_RIDERS:2 * N_RIDERS + 1]
    q_s, k_s, b_s, qbar_s, kbar_s, dec_s, kbarc_s, decc_s, ob_s = refs[2 * N_RIDERS + 1:]
    for w_f32, w_bf16 in zip(cast_in, cast_out):
        w_bf16[...] = w_f32[...].astype(BF16)

    lb = lb_ref[...]
    lower = []
    for d in range(2):
        l0, l1 = lb[d, 0:1], lb[d, 1:2]
        mx = jnp.maximum(l0, l1)
        e0, e1 = jnp.exp(l0 - mx), jnp.exp(l1 - mx)
        lower.append(e0 / (e0 + e1))
    f_lat = (ff_ref, fb_ref)
    f_ctx = (ffc_ref, fbc_ref)

    for d in range(2):
        k, b, b_tot, tots = _prep(f_ctx[d][0], lower[d], d == 1)
        kbarc_s[d] = (k * jnp.exp2(b_tot - b)).astype(BF16)
        for c in range(CTX_CHUNKS):
            decc_s[d, c] = jnp.broadcast_to(jnp.exp2(tots[c]), (8, HEAD_W))

    def prep_body(g, carry):
        rows = pl.ds(pl.multiple_of(g * PREP_ROWS, PREP_ROWS), PREP_ROWS)
        qr = q_ref[0, rows, :]
        q = qr * _sigmoid(qr)
        q_s[rows, :] = q
        for d in range(2):
            k, b, b_tot, tots = _prep(f_lat[d][0, rows, :], lower[d], d == 1)
            k_s[d, rows, :] = k
            b_s[d, rows, :] = b
            qbar_s[d, rows, :] = (q * jnp.exp2(b)).astype(BF16)
            kbar_s[d, rows, :] = (k * jnp.exp2(b_tot - b)).astype(BF16)
            for i in range(PREP_ROWS // CHUNK):
                dec_s[d, g * (PREP_ROWS // CHUNK) + i] = jnp.broadcast_to(jnp.exp2(tots[i]), (8, HEAD_W))
        return carry

    lax.fori_loop(0, SEQ // PREP_ROWS, prep_body, 0)

    def chunk_scores(base, with_diag):
        rows = pl.ds(base, CHUNK)
        out = []
        for d in range(2):
            a = _block_scores(q_s[rows, :], k_s[d, rows, :], b_s[d, rows, :], d == 1, with_diag)
            if with_diag:
                r = lax.broadcasted_iota(jnp.int32, a.shape, 0)
                c = lax.broadcasted_iota(jnp.int32, a.shape, 1)
                a = jnp.where((c >= r) if d == 1 else (c <= r), a, 0.0)
            out.append(a.astype(BF16))
        return out

    def value_products(bases, scores):
        pv = []
        for j, base in enumerate(bases):
            v16 = v_ref[0, pl.ds(base, CHUNK), :].astype(BF16)
            pv.append(jnp.dot(scores[j][0], v16, preferred_element_type=F32)
                      + jnp.dot(scores[j][1], v16, preferred_element_type=F32))
        return pv

    def intra_body(g, carry):
        bases = [pl.multiple_of((g * INTRA_GROUP + j) * CHUNK, CHUNK) for j in range(INTRA_GROUP)]

        def diag(base):
            blocks = []
            for i in range(CHUNK // SUB):
                blk = pl.multiple_of(base + SUB * i, SUB)
                blocks.append(_diag_block(q_s, k_s.at[0], b_s.at[0], v_ref, blk, False)
                              + _diag_block(q_s, k_s.at[1], b_s.at[1], v_ref, blk, True))
            return jnp.concatenate(blocks, axis=0)

        scores = [chunk_scores(base, False) for base in bases]
        od = [diag(bases[0])]
        pv = value_products(bases, scores)
        od += [diag(base) for base in bases[1:]]
        for j, base in enumerate(bases):
            rows = pl.ds(base, CHUNK)
            o_ref[0, rows, :] = (o_ref[0, rows, :] + ob_s[rows, :]) + (pv[j] + od[j])
        return carry

    def intra_body_bounded(g, carry):
        bases = [pl.multiple_of((g * BOUNDED_GROUP + j) * CHUNK, CHUNK) for j in range(BOUNDED_GROUP)]
        pv = value_products(bases, [chunk_scores(base, True) for base in bases])
        for j, base in enumerate(bases):
            rows = pl.ds(base, CHUNK)
            o_ref[0, rows, :] = (o_ref[0, rows, :] + ob_s[rows, :]) + pv[j]
        return carry

    def ctx_states():
        steps = [(d, (CTX_CHUNKS - 1 - c) if d == 1 else c) for c in range(CTX_CHUNKS) for d in range(2)]
        upd = [lax.dot_general(vc_ref[0, cc * CHUNK:(cc + 1) * CHUNK, :].astype(BF16),
                               kbarc_s[d, cc * CHUNK:(cc + 1) * CHUNK, :], (((0,), (0,)), ((), ())),
                               preferred_element_type=F32) for d, cc in steps]
        sts = [None, None]
        for n, (d, cc) in enumerate(steps):
            sts[d] = upd[n] if sts[d] is None else decc_s[d, cc][0:1] * sts[d] + upd[n]
        return tuple(sts)

    def lat_body(it, sts):
        sts = list(sts)
        steps = []
        for u in range(SCAN_UNROLL):
            c = it * SCAN_UNROLL + u
            for d in range(2):
                cc = (LAT_CHUNKS - 1 - c) if d == 1 else c
                steps.append((d, cc, pl.ds(pl.multiple_of(cc * CHUNK, CHUNK), CHUNK)))
        upd = [lax.dot_general(v_ref[0, rows, :].astype(BF16), kbar_s[d, rows, :], (((0,), (0,)), ((), ())),
                               preferred_element_type=F32) for d, cc, rows in steps]
        for n, (d, cc, rows) in enumerate(steps):
            inter = lax.dot_general(qbar_s[d, rows, :], sts[d].astype(BF16), (((1,), (1,)), ((), ())),
                                    preferred_element_type=F32)
            if d == 0:
                o_ref[0, rows, :] = inter
            else:
                ob_s[rows, :] = inter
            sts[d] = dec_s[d, cc][0:1] * sts[d] + upd[n]
        return tuple(sts)

    lax.fori_loop(0, LAT_CHUNKS // SCAN_UNROLL, lat_body, ctx_states())
    worst = jnp.min(jnp.minimum(lower[0], lower[1]))
    bounded = worst >= 2.0 ** (-DIAG_LOG2_BOUND / SUB)

    @pl.when(bounded)
    def _():
        lax.fori_loop(0, LAT_CHUNKS // BOUNDED_GROUP, intra_body_bounded, 0)

    @pl.when(jnp.logical_not(bounded))
    def _():
        lax.fori_loop(0, LAT_CHUNKS // INTRA_GROUP, intra_body, 0)


def _hgrn2_call(z3, zc3, rec_lb, riders):
    lat = lambda col: pl.BlockSpec((1, SEQ, HEAD_W), lambda b, h: (b, 0, col + h))
    ctx = lambda col: pl.BlockSpec((1, CTX_LEN, HEAD_W), lambda b, h: (b, 0, col + h))
    assert len(riders) == N_RIDERS
    slab = lambda w: pl.BlockSpec((w.shape[0] // (BATCH * HEADS), w.shape[1]), lambda b, h: (b * HEADS + h, 0))
    return pl.pallas_call(
        _hgrn2_kernel,
        grid=(BATCH, HEADS),
        in_specs=[
            pl.BlockSpec((2, 2, HEAD_W), lambda b, h: (0, 0, h)),
            lat(COL_RFF), lat(COL_RFB), lat(COL_RI), lat(COL_RQ),
            ctx(COL_RFF), ctx(COL_RFB), ctx(COL_RI),
        ] + [slab(w) for w in riders],
        out_specs=[pl.BlockSpec((1, SEQ, HEAD_W), lambda b, h: (b, 0, h))] + [slab(w) for w in riders],
        out_shape=[jax.ShapeDtypeStruct((BATCH, SEQ, REC_WIDTH), F32)]
        + [jax.ShapeDtypeStruct(w.shape, BF16) for w in riders],
        scratch_shapes=[
            pltpu.VMEM((SEQ, HEAD_W), F32),
            pltpu.VMEM((2, SEQ, HEAD_W), F32),
            pltpu.VMEM((2, SEQ, HEAD_W), F32),
            pltpu.VMEM((2, SEQ, HEAD_W), BF16),
            pltpu.VMEM((2, SEQ, HEAD_W), BF16),
            pltpu.VMEM((2, LAT_CHUNKS, 8, HEAD_W), F32),
            pltpu.VMEM((2, CTX_LEN, HEAD_W), BF16),
            pltpu.VMEM((2, CTX_CHUNKS, 8, HEAD_W), F32),
            pltpu.VMEM((SEQ, HEAD_W), F32),
        ],
        compiler_params=_cparams(2, VMEM_LIMIT),
        name="hgrn2",
    )(rec_lb, z3, z3, z3, z3, zc3, zc3, zc3, *riders)


def _merge_kernel(att_ref, orec_ref, rg_ref, ga_ref, gr_ref, x_ref, g1_ref, sh2_ref, sc2_ref,
                  gnw_ref, n2w_ref, wba_ref, wbr_ref, wout_ref, x1_ref, h2_ref):
    rg = rg_ref[...]
    rec = _rms(orec_ref[...], gnw_ref[...]) * (rg * _sigmoid(rg))
    ya = jnp.dot(att_ref[...], wba_ref[...], preferred_element_type=F32)
    yr = jnp.dot(rec.astype(BF16), wbr_ref[...], preferred_element_type=F32)
    y = _sigmoid(ga_ref[...]) * ya + _sigmoid(gr_ref[...]) * yr
    x1 = x_ref[...] + g1_ref[0] * jnp.dot(y.astype(BF16), wout_ref[...], preferred_element_type=F32)
    x1_ref[...] = x1
    h2 = _rms(x1, n2w_ref[...]) * (1.0 + sc2_ref[0]) + sh2_ref[0]
    h2_ref[...] = h2.astype(BF16)


def _merge_call(att2d, orec2d, z2d, x2d, mod3, gnorm_w, norm2_w, wba, wbr, wout):
    tm = MERGE_TM
    m = x2d.shape[0]
    row = lambda i: (i, 0)
    const = lambda i: (0, 0)
    modspec = lambda k: pl.BlockSpec((1, 1, D_MODEL), lambda i: ((i * tm) // SEQ, 0, k))
    resident = lambda shape: pl.BlockSpec(shape, const, pipeline_mode=pl.Buffered(1))
    return pl.pallas_call(
        _merge_kernel,
        grid=(m // tm,),
        in_specs=[
            pl.BlockSpec((tm, ATTN_WIDTH), row),
            pl.BlockSpec((tm, REC_WIDTH), row),
            pl.BlockSpec((tm, REC_WIDTH), lambda i: (i, 7)),
            pl.BlockSpec((tm, D_MODEL), lambda i: (i, 4)),
            pl.BlockSpec((tm, D_MODEL), lambda i: (i, 5)),
            pl.BlockSpec((tm, D_MODEL), row),
            modspec(2), modspec(3), modspec(4),
            pl.BlockSpec((1, REC_WIDTH), const),
            pl.BlockSpec((1, D_MODEL), const),
            resident((ATTN_WIDTH, D_MODEL)),
            resident((REC_WIDTH, D_MODEL)),
            resident((D_MODEL, D_MODEL)),
        ],
        out_specs=[pl.BlockSpec((tm, D_MODEL), row), pl.BlockSpec((tm, D_MODEL), row)],
        out_shape=[jax.ShapeDtypeStruct((m, D_MODEL), F32), jax.ShapeDtypeStruct((m, D_MODEL), BF16)],
        compiler_params=_cparams(1, VMEM_LIMIT),
        name="merge",
    )(att2d, orec2d, z2d, z2d, z2d, x2d, mod3, mod3, mod3, gnorm_w, norm2_w, wba, wbr, wout)


FFN_PAD = 8


def _ffn_up_kernel(h_ref, wa_ref, wb_ref, cwa_ref, cwb_ref, cba_ref, cbb_ref, wd_ref, g_ref, wd16_ref, u_s):
    wd16_ref[...] = wd_ref[...].astype(BF16)
    h = h_ref[0]
    n_units = wa_ref.shape[1] // FFN_COLS
    units = []
    for c in range(n_units):
        cols = slice(c * FFN_COLS, (c + 1) * FFN_COLS)
        units.append((wa_ref, cwa_ref, cba_ref, cols))
        units.append((wb_ref, cwb_ref, cbb_ref, cols))

    pad = jnp.zeros((FFN_PAD, FFN_COLS), F32)
    for slot in range(2):
        u_s[slot, 0:FFN_PAD, :] = pad
        u_s[slot, FFN_PAD + SEQ:2 * FFN_PAD + SEQ, :] = pad

    def proj(n):
        w_ref, _, _, cols = units[n]
        u_s[n % 2, FFN_PAD:FFN_PAD + SEQ, :] = jnp.dot(h, w_ref[:, cols], preferred_element_type=F32)

    def conv(n):
        _, cw_ref, cb_ref, cols = units[n]
        cw = cw_ref[:, cols]
        buf = u_s.at[n % 2]
        return cb_ref[:, cols] + (buf[FFN_PAD - 1:FFN_PAD - 1 + SEQ, :] * cw[0:1]
                                  + buf[FFN_PAD:FFN_PAD + SEQ, :] * cw[1:2]
                                  + buf[FFN_PAD + 1:FFN_PAD + 1 + SEQ, :] * cw[2:3])

    proj(0)
    a = None
    for n in range(len(units)):
        if n + 1 < len(units):
            proj(n + 1)
        y = conv(n)
        if n % 2 == 0:
            a = y * _sigmoid(y)
        else:
            g_ref[0, :, units[n][3]] = (a * y).astype(BF16)


def _ffn_up_call(h3, w_up, conv_w, conv_b, w_down):
    tn = FFN_TN
    nb = FFN_DIM // tn
    wd_slab = pl.BlockSpec((FFN_DIM // (BATCH * nb), D_MODEL), lambda b, j: (b * nb + j, 0))
    return pl.pallas_call(
        _ffn_up_kernel,
        grid=(BATCH, nb),
        in_specs=[
            pl.BlockSpec((1, SEQ, D_MODEL), lambda b, j: (b, 0, 0)),
            pl.BlockSpec((D_MODEL, tn), lambda b, j: (0, j)),
            pl.BlockSpec((D_MODEL, tn), lambda b, j: (0, nb + j)),
            pl.BlockSpec((3, tn), lambda b, j: (0, j)),
            pl.BlockSpec((3, tn), lambda b, j: (0, nb + j)),
            pl.BlockSpec((1, tn), lambda b, j: (0, j)),
            pl.BlockSpec((1, tn), lambda b, j: (0, nb + j)),
            wd_slab,
        ],
        out_specs=[pl.BlockSpec((1, SEQ, tn), lambda b, j: (b, 0, j)), wd_slab],
        out_shape=[jax.ShapeDtypeStruct((BATCH, SEQ, FFN_DIM), BF16),
                   jax.ShapeDtypeStruct(w_down.shape, BF16)],
        scratch_shapes=[pltpu.VMEM((2, SEQ + 2 * FFN_PAD, FFN_COLS), F32)],
        compiler_params=_cparams(2, VMEM_LIMIT),
        name="ffn_up",
    )(h3, w_up, w_up, conv_w, conv_w, conv_b, conv_b, w_down)


def _ffn_down_kernel(g_ref, w_ref, x1_ref, g2_ref, fw_ref, o_ref):
    y = jnp.dot(g_ref[...], w_ref[...], preferred_element_type=F32)
    x2 = x1_ref[...] + g2_ref[0] * y
    o_ref[...] = _rms(x2, fw_ref[...])


def _ffn_down_call(g2d, w_down, x1, mod3, final_w):
    tm = FFN_DOWN_TM
    m = g2d.shape[0]
    return pl.pallas_call(
        _ffn_down_kernel,
        grid=(m // tm,),
        in_specs=[
            pl.BlockSpec((tm, FFN_DIM), lambda i: (i, 0)),
            pl.BlockSpec((FFN_DIM, D_MODEL), lambda i: (0, 0), pipeline_mode=pl.Buffered(1)),
            pl.BlockSpec((tm, D_MODEL), lambda i: (i, 0)),
            pl.BlockSpec((1, 1, D_MODEL), lambda i: ((i * tm) // SEQ, 0, 5)),
            pl.BlockSpec((1, D_MODEL), lambda i: (0, 0)),
        ],
        out_specs=pl.BlockSpec((tm, D_MODEL), lambda i: (i, 0)),
        out_shape=jax.ShapeDtypeStruct((m, D_MODEL), F32),
        compiler_params=_cparams(1, VMEM_LIMIT),
        name="ffn_down",
    )(g2d, w_down, x1, mod3, final_w)


def _rope_tables():
    rows = SEQ // GRID_W
    r, col = jnp.meshgrid(jnp.arange(rows), jnp.arange(GRID_W), indexing="ij")
    pos = jnp.stack([r.reshape(-1), col.reshape(-1)], axis=-1).astype(F32)
    nq = HEAD_DIM // 4
    inv = ROPE_BASE ** (-jnp.arange(nq, dtype=F32) / nq)
    ang = pos[:, :, None] * inv
    cos, sin = jnp.cos(ang), jnp.sin(ang)
    lane = jnp.arange(HEAD_W)
    axis = (lane % HEAD_DIM) // (2 * nq)
    second = ((lane % (2 * nq)) // nq) == 1
    freq = lane % nq
    c_t = cos[:, axis, freq]
    s_t = sin[:, axis, freq]
    sa_t = jnp.where(second[None, :], 0.0, -s_t)
    sb_t = jnp.where(second[None, :], s_t, 0.0)
    return c_t, sa_t, sb_t


def kernel(x, c, ctx, c_ctx, w_mod, b_mod, norm1_w, w_in, lam_q1, lam_k1, lam_q2, lam_k2, subln_w,
           rec_lb, rec_gnorm_w, w_branch_attn, w_branch_rec, w_out, norm2_w, w_up, conv_w, conv_b,
           w_down, final_norm_w):
    m_lat = BATCH * SEQ
    c_all = jnp.concatenate([c, c_ctx[None, :], jnp.zeros((MOD_ROWS - BATCH - 1, D_MODEL), F32)], axis=0)
    mod = _mod_call(c_all, w_mod[0], b_mod[0][None, :])
    mod3 = mod.reshape(MOD_ROWS, 1, N_MOD * D_MODEL)

    w_in16 = w_in[0].astype(BF16)
    n1w = norm1_w[0][None, :]
    x2d = x.reshape(m_lat, D_MODEL)
    z = _inproj_call(x2d, n1w, mod3, w_in16, IN_WIDTH, lambda i: (i * INPROJ_TM) // SEQ, INPROJ_TM, "inproj")
    zc = _inproj_call(ctx.reshape(BATCH * CTX_LEN, D_MODEL), n1w, mod3, w_in16, CTX_KV_WIDTH,
                      lambda i: CTX_MOD_ROW, INPROJ_TM, "inproj_ctx")
    z3 = z.reshape(BATCH, SEQ, IN_WIDTH)
    zc3 = zc.reshape(BATCH, CTX_LEN, CTX_KV_WIDTH)

    lam_params = jnp.stack([lam_q1[0], lam_k1[0], lam_q2[0], lam_k2[0]], axis=0)
    att = _attn_call(z3, zc3, lam_params, _rope_tables(), subln_w[0][None, :])
    orec, w_up16, w_out16, w_ba16, w_br16 = _hgrn2_call(
        z3, zc3, rec_lb, [w_up[0], w_out[0], w_branch_attn[0], w_branch_rec[0]])

    x1, h2 = _merge_call(att.reshape(m_lat, ATTN_WIDTH), orec.reshape(m_lat, REC_WIDTH), z, x2d, mod3,
                         rec_gnorm_w[0][None, :], norm2_w[0][None, :],
                         w_ba16, w_br16, w_out16)
    g, w_down16 = _ffn_up_call(h2.reshape(BATCH, SEQ, D_MODEL), w_up16, conv_w[0], conv_b[0][None, :], w_down[0])
    out = _ffn_down_call(g.reshape(m_lat, FFN_DIM), w_down16, x1, mod3, final_norm_w[None, :])
    return out.reshape(BATCH, SEQ, D_MODEL)
```

```python
import math

import jax
import jax.numpy as jnp
from jax import lax
from jax.experimental import pallas as pl
from jax.experimental.pallas import tpu as pltpu

F32 = jnp.float32
BF16 = jnp.bfloat16

D_MODEL = 2048
BATCH = 8
SEQ = 2048
GRID_W = 64
CTX_LEN = 256
EPS = 1e-6
N_MOD = 6
HEADS = 8
HEAD_DIM = 64
HEAD_W = 128
ATTN_WIDTH = 1024
REC_WIDTH = 1024
FFN_DIM = 5632
IN_WIDTH = 12288
CTX_KV_WIDTH = 5120
ROPE_BASE = 10000.0
LAM_INIT = 0.8 - 0.6 * math.exp(-0.3 * 0)

COL_AK, COL_AV, COL_RFF, COL_RFB, COL_RI, COL_AQ, COL_RQ = 0, 8, 16, 24, 32, 40, 48
MOD_ROWS = 16
CTX_MOD_ROW = BATCH

CHUNK = 64
SUB = 16
LOG2E = 1.4426950408889634

VMEM_LIMIT = 56 * 1024 * 1024
MOD_TN = 1024
INPROJ_TM = 1024
INPROJ_TN, INPROJ_CTX_TN = 1536, 1024
ATTN_SUB = 512
MERGE_TM = 256
FFN_TN = 512
FFN_COLS = 256
FFN_DOWN_TM = 256


def _cparams(n_axes, vmem=None):
    return pltpu.CompilerParams(
        dimension_semantics=("arbitrary",) * n_axes,
        vmem_limit_bytes=vmem,
    )


def _sigmoid(x):
    return 1.0 / (1.0 + jnp.exp2(x * (-LOG2E)))


def _rms(x, w):
    return x * lax.rsqrt(jnp.mean(x * x, axis=-1, keepdims=True) + EPS) * w


def _mod_kernel(c_ref, w_ref, b_ref, o_ref):
    c = c_ref[...]
    a = (c * _sigmoid(c)).astype(BF16)
    o_ref[...] = jnp.dot(a, w_ref[...].astype(BF16), preferred_element_type=F32) + b_ref[...]


def _mod_call(c_all, w_mod, b_mod):
    tn = MOD_TN
    n = w_mod.shape[1]
    return pl.pallas_call(
        _mod_kernel,
        grid=(n // tn,),
        in_specs=[
            pl.BlockSpec((MOD_ROWS, D_MODEL), lambda j: (0, 0)),
            pl.BlockSpec((D_MODEL, tn), lambda j: (0, j)),
            pl.BlockSpec((1, tn), lambda j: (0, j)),
        ],
        out_specs=pl.BlockSpec((MOD_ROWS, tn), lambda j: (0, j)),
        out_shape=jax.ShapeDtypeStruct((MOD_ROWS, n), F32),
        compiler_params=_cparams(1, VMEM_LIMIT),
        name="mod",
    )(c_all, w_mod, b_mod)


def _inproj_kernel(x_ref, nw_ref, sh_ref, sc_ref, w_ref, o_ref, h_ref):
    @pl.when(pl.program_id(1) == 0)
    def _():
        h = _rms(x_ref[...], nw_ref[...]) * (1.0 + sc_ref[0]) + sh_ref[0]
        h_ref[...] = h.astype(BF16)

    o_ref[...] = jnp.dot(h_ref[...], w_ref[...], preferred_element_type=F32)


def _inproj_call(x2d, norm_w, mod3, w_bf16, n_out, mod_row_of_tile, tm, tn, name):
    m = x2d.shape[0]
    return pl.pallas_call(
        _inproj_kernel,
        grid=(m // tm, n_out // tn),
        in_specs=[
            pl.BlockSpec((tm, D_MODEL), lambda i, j: (i, 0)),
            pl.BlockSpec((1, D_MODEL), lambda i, j: (0, 0)),
            pl.BlockSpec((1, 1, D_MODEL), lambda i, j: (mod_row_of_tile(i), 0, 0)),
            pl.BlockSpec((1, 1, D_MODEL), lambda i, j: (mod_row_of_tile(i), 0, 1)),
            pl.BlockSpec((D_MODEL, tn), lambda i, j: (0, j)),
        ],
        out_specs=pl.BlockSpec((tm, tn), lambda i, j: (i, j)),
        out_shape=jax.ShapeDtypeStruct((m, n_out), F32),
        scratch_shapes=[pltpu.VMEM((tm, D_MODEL), BF16)],
        compiler_params=_cparams(2, VMEM_LIMIT),
        name=name,
    )(x2d, norm_w, mod3, mod3, w_bf16)


def _rope(x, c, sa, sb):
    return x * c + pltpu.roll(x, HEAD_W - 16, 1) * sa + pltpu.roll(x, 16, 1) * sb


def _attn_kernel(lamp_ref, q_ref, k_ref, v_ref, kc_ref, vc_ref, c_ref, sa_ref, sb_ref, subln_ref, o_ref, k_s, v_s):
    n_sub = SEQ // ATTN_SUB
    c, sa, sb = c_ref[...], sa_ref[...], sb_ref[...]

    k_s[0:SEQ, :] = _rope(k_ref[0], c, sa, sb).astype(BF16)
    k_s[SEQ:SEQ + CTX_LEN, :] = kc_ref[0].astype(BF16)
    v_s[0:SEQ, 0:HEAD_W] = v_ref[0].astype(BF16)
    v_s[SEQ:SEQ + CTX_LEN, 0:HEAD_W] = vc_ref[0].astype(BF16)
    v_s[:, HEAD_W:2 * HEAD_W] = jnp.ones((SEQ + CTX_LEN, HEAD_W), BF16)

    lp = lamp_ref[...]
    lam = (jnp.exp(jnp.sum(lp[0:1] * lp[1:2], axis=-1, keepdims=True))
           - jnp.exp(jnp.sum(lp[2:3] * lp[3:4], axis=-1, keepdims=True)) + LAM_INIT)

    q = _rope(q_ref[0], c, sa, sb) * (HEAD_DIM ** -0.5 * LOG2E)
    lane = lax.broadcasted_iota(jnp.int32, q.shape, 1)
    q0 = jnp.where(lane < HEAD_DIM, q, 0.0).astype(BF16)
    q1 = jnp.where(lane >= HEAD_DIM, q, 0.0).astype(BF16)

    def scores(i):
        lo, hi = i * ATTN_SUB, (i + 1) * ATTN_SUB
        qq = jnp.concatenate([q0[lo:hi], q1[lo:hi]], axis=0)
        return lax.dot_general(qq, k_s[...], (((1,), (1,)), ((), ())), preferred_element_type=F32)

    s_next = scores(0)
    for i in range(n_sub):
        s = s_next
        if i + 1 < n_sub:
            s_next = scores(i + 1)
        e = jnp.exp2(s - jnp.max(s, axis=-1, keepdims=True)).astype(BF16)
        r = jnp.dot(e, v_s[...], preferred_element_type=F32)
        p = r[:, 0:HEAD_W] / r[:, HEAD_W:2 * HEAD_W]
        o = p[0:ATTN_SUB] - lam * p[ATTN_SUB:2 * ATTN_SUB]
        o = _rms(o, subln_ref[...]) * (1.0 - LAM_INIT)
        o_ref[0, i * ATTN_SUB:(i + 1) * ATTN_SUB, :] = o.astype(BF16)


def _attn_call(z3, zc3, lam_params, rope_tabs, subln_w):
    lat = lambda col: pl.BlockSpec((1, SEQ, HEAD_W), lambda b, h: (b, 0, col + h))
    ctx = lambda col: pl.BlockSpec((1, CTX_LEN, HEAD_W), lambda b, h: (b, 0, col + h))
    full = lambda shape: pl.BlockSpec(shape, lambda b, h: (0, 0))
    return pl.pallas_call(
        _attn_kernel,
        grid=(BATCH, HEADS),
        in_specs=[full((4, HEAD_DIM)), lat(COL_AQ), lat(COL_AK), lat(COL_AV), ctx(COL_AK), ctx(COL_AV),
                  full((SEQ, HEAD_W)), full((SEQ, HEAD_W)), full((SEQ, HEAD_W)), full((1, HEAD_W))],
        out_specs=pl.BlockSpec((1, SEQ, HEAD_W), lambda b, h: (b, 0, h)),
        out_shape=jax.ShapeDtypeStruct((BATCH, SEQ, ATTN_WIDTH), BF16),
        scratch_shapes=[pltpu.VMEM((SEQ + CTX_LEN, HEAD_W), BF16),
                        pltpu.VMEM((SEQ + CTX_LEN, 2 * HEAD_W), BF16)],
        compiler_params=_cparams(2, VMEM_LIMIT),
        name="attn",
    )(lam_params, z3, z3, z3, zc3, zc3, *rope_tabs, subln_w)


LAT_CHUNKS = SEQ // CHUNK
CTX_CHUNKS = CTX_LEN // CHUNK
PREP_ROWS = 1024
INTRA_GROUP = 2
BOUNDED_GROUP = 32
DIAG_LOG2_BOUND = 96.0
SCAN_UNROLL = 32


def _tri(rev):
    r = lax.broadcasted_iota(jnp.int32, (CHUNK, CHUNK), 0)
    c = lax.broadcasted_iota(jnp.int32, (CHUNK, CHUNK), 1)
    return jnp.where((c >= r) if rev else (c <= r), 1.0, 0.0).astype(F32)


def _gate(f_raw, lower):
    f = lower + (1.0 - lower) * _sigmoid(f_raw)
    return 1.0 - f, jnp.log2(f)


def _prep(f_raw, lower, rev):
    n = f_raw.shape[0] // CHUNK
    k, logf = _gate(f_raw, lower)
    wide = jnp.concatenate([logf[i * CHUNK:(i + 1) * CHUNK] for i in range(n)], axis=1)
    tri = _tri(rev).astype(BF16)
    hi = wide.astype(BF16)
    rest = wide - hi.astype(F32)
    mid = rest.astype(BF16)
    lo = (rest - mid.astype(F32)).astype(BF16)
    bw = (jnp.dot(tri, hi, preferred_element_type=F32) + jnp.dot(tri, mid, preferred_element_type=F32)
          + jnp.dot(tri, lo, preferred_element_type=F32))
    t = 0 if rev else CHUNK - 1
    b = jnp.concatenate([bw[:, i * HEAD_W:(i + 1) * HEAD_W] for i in range(n)], axis=0)
    tots = [bw[t:t + 1, i * HEAD_W:(i + 1) * HEAD_W] for i in range(n)]
    b_tot = jnp.concatenate([jnp.broadcast_to(r, (CHUNK, HEAD_W)) for r in tots], axis=0)
    return k, b, b_tot, tots


def _diag_block(q_s, k_d, b_d, v_ref, base, rev):
    rows = lax.broadcasted_iota(jnp.int32, (8, HEAD_W), 0)
    halves = range(SUB // 8)
    qh = [q_s[pl.ds(base + 8 * u, 8), :] for u in halves]
    bh = [b_d[pl.ds(base + 8 * u, 8), :] for u in halves]
    acc = [jnp.zeros((8, HEAD_W), F32) for _ in halves]
    for s in range(SUB):
        bs = b_d[pl.ds(base + s, 1), :]
        ks = k_d[pl.ds(base + s, 1), :]
        vs = v_ref[0, pl.ds(base + s, 1), :]
        for u in halves:
            t_lo, t_hi = 8 * u, 8 * u + 7
            if (t_lo > s) if rev else (t_hi < s):
                continue
            w = jnp.exp2(bh[u] - bs)
            if (t_hi > s) if rev else (t_lo < s):
                keep = (rows + t_lo <= s) if rev else (rows + t_lo >= s)
                w = jnp.where(keep, w, 0.0)
            col = jnp.sum(qh[u] * w * ks, axis=-1, keepdims=True)
            acc[u] = acc[u] + col * vs
    return jnp.concatenate(acc, axis=0)


def _block_scores(q, k, b, rev, with_diag):
    blocks = []
    for i in range(CHUNK // SUB):
        lo, hi = SUB * i, SUB * (i + 1)
        if rev:
            e_lo, e_hi = (lo if with_diag else hi), CHUNK
            ref_row = b[hi:hi + 1] if hi < CHUNK else None
        else:
            e_lo, e_hi = 0, (hi if with_diag else lo)
            ref_row = b[lo - 1:lo] if lo > 0 else None
        if ref_row is None:
            if not with_diag:
                blocks.append(jnp.zeros((SUB, CHUNK), F32))
                continue
            ref_row = jnp.zeros((1, HEAD_W), F32)
        qt = (q[lo:hi] * jnp.exp2(b[lo:hi] - ref_row)).astype(BF16)
        kt = (k[e_lo:e_hi] * jnp.exp2(ref_row - b[e_lo:e_hi])).astype(BF16)
        parts = []
        if e_lo > 0:
            parts.append(jnp.zeros((e_lo, HEAD_W), BF16))
        parts.append(kt)
        if e_hi < CHUNK:
            parts.append(jnp.zeros((CHUNK - e_hi, HEAD_W), BF16))
        ktp = jnp.concatenate(parts, axis=0)
        blocks.append(lax.dot_general(qt, ktp, (((1,), (1,)), ((), ())), preferred_element_type=F32))
    return jnp.concatenate(blocks, axis=0)


N_RIDERS = 4


def _hgrn2_kernel(lb_ref, ff_ref, fb_ref, v_ref, q_ref, ffc_ref, fbc_ref, vc_ref, *refs):
    cast_in, (o_ref, *cast_out) = refs[:N_RIDERS], refs[N_RIDERS:2 * N_RIDERS + 1]
    q_s, k_s, b_s, qbar_s, kbar_s, dec_s, kbarc_s, decc_s, ob_s = refs[2 * N_RIDERS + 1:]
    for w_f32, w_bf16 in zip(cast_in, cast_out):
        w_bf16[...] = w_f32[...].astype(BF16)

    lb = lb_ref[...]
    lower = []
    for d in range(2):
        l0, l1 = lb[d, 0:1], lb[d, 1:2]
        mx = jnp.maximum(l0, l1)
        e0, e1 = jnp.exp(l0 - mx), jnp.exp(l1 - mx)
        lower.append(e0 / (e0 + e1))
    f_lat = (ff_ref, fb_ref)
    f_ctx = (ffc_ref, fbc_ref)

    for d in range(2):
        k, b, b_tot, tots = _prep(f_ctx[d][0], lower[d], d == 1)
        kbarc_s[d] = (k * jnp.exp2(b_tot - b)).astype(BF16)
        for c in range(CTX_CHUNKS):
            decc_s[d, c] = jnp.broadcast_to(jnp.exp2(tots[c]), (8, HEAD_W))

    def prep_body(g, carry):
        rows = pl.ds(pl.multiple_of(g * PREP_ROWS, PREP_ROWS), PREP_ROWS)
        qr = q_ref[0, rows, :]
        q = qr * _sigmoid(qr)
        q_s[rows, :] = q
        for d in range(2):
            k, b, b_tot, tots = _prep(f_lat[d][0, rows, :], lower[d], d == 1)
            k_s[d, rows, :] = k
            b_s[d, rows, :] = b
            qbar_s[d, rows, :] = (q * jnp.exp2(b)).astype(BF16)
            kbar_s[d, rows, :] = (k * jnp.exp2(b_tot - b)).astype(BF16)
            for i in range(PREP_ROWS // CHUNK):
                dec_s[d, g * (PREP_ROWS // CHUNK) + i] = jnp.broadcast_to(jnp.exp2(tots[i]), (8, HEAD_W))
        return carry

    lax.fori_loop(0, SEQ // PREP_ROWS, prep_body, 0)

    def chunk_scores(base, with_diag):
        rows = pl.ds(base, CHUNK)
        out = []
        for d in range(2):
            a = _block_scores(q_s[rows, :], k_s[d, rows, :], b_s[d, rows, :], d == 1, with_diag)
            if with_diag:
                r = lax.broadcasted_iota(jnp.int32, a.shape, 0)
                c = lax.broadcasted_iota(jnp.int32, a.shape, 1)
                a = jnp.where((c >= r) if d == 1 else (c <= r), a, 0.0)
            out.append(a.astype(BF16))
        return out

    def value_products(bases, scores):
        pv = []
        for j, base in enumerate(bases):
            v16 = v_ref[0, pl.ds(base, CHUNK), :].astype(BF16)
            pv.append(jnp.dot(scores[j][0], v16, preferred_element_type=F32)
                      + jnp.dot(scores[j][1], v16, preferred_element_type=F32))
        return pv

    def intra_body(g, carry):
        bases = [pl.multiple_of((g * INTRA_GROUP + j) * CHUNK, CHUNK) for j in range(INTRA_GROUP)]

        def diag(base):
            blocks = []
            for i in range(CHUNK // SUB):
                blk = pl.multiple_of(base + SUB * i, SUB)
                blocks.append(_diag_block(q_s, k_s.at[0], b_s.at[0], v_ref, blk, False)
                              + _diag_block(q_s, k_s.at[1], b_s.at[1], v_ref, blk, True))
            return jnp.concatenate(blocks, axis=0)

        scores = [chunk_scores(base, False) for base in bases]
        od = [diag(bases[0])]
        pv = value_products(bases, scores)
        od += [diag(base) for base in bases[1:]]
        for j, base in enumerate(bases):
            rows = pl.ds(base, CHUNK)
            o_ref[0, rows, :] = (o_ref[0, rows, :] + ob_s[rows, :]) + (pv[j] + od[j])
        return carry

    def intra_body_bounded(g, carry):
        bases = [pl.multiple_of((g * BOUNDED_GROUP + j) * CHUNK, CHUNK) for j in range(BOUNDED_GROUP)]
        pv = value_products(bases, [chunk_scores(base, True) for base in bases])
        for j, base in enumerate(bases):
            rows = pl.ds(base, CHUNK)
            o_ref[0, rows, :] = (o_ref[0, rows, :] + ob_s[rows, :]) + pv[j]
        return carry

    def ctx_states():
        steps = [(d, (CTX_CHUNKS - 1 - c) if d == 1 else c) for c in range(CTX_CHUNKS) for d in range(2)]
        upd = [lax.dot_general(vc_ref[0, cc * CHUNK:(cc + 1) * CHUNK, :].astype(BF16),
                               kbarc_s[d, cc * CHUNK:(cc + 1) * CHUNK, :], (((0,), (0,)), ((), ())),
                               preferred_element_type=F32) for d, cc in steps]
        sts = [None, None]
        for n, (d, cc) in enumerate(steps):
            sts[d] = upd[n] if sts[d] is None else decc_s[d, cc][0:1] * sts[d] + upd[n]
        return tuple(sts)

    def lat_body(it, sts):
        sts = list(sts)
        steps = []
        for u in range(SCAN_UNROLL):
            c = it * SCAN_UNROLL + u
            for d in range(2):
                cc = (LAT_CHUNKS - 1 - c) if d == 1 else c
                steps.append((d, cc, pl.ds(pl.multiple_of(cc * CHUNK, CHUNK), CHUNK)))
        upd = [lax.dot_general(v_ref[0, rows, :].astype(BF16), kbar_s[d, rows, :], (((0,), (0,)), ((), ())),
                               preferred_element_type=F32) for d, cc, rows in steps]
        for n, (d, cc, rows) in enumerate(steps):
            inter = lax.dot_general(qbar_s[d, rows, :], sts[d].astype(BF16), (((1,), (1,)), ((), ())),
                                    preferred_element_type=F32)
            if d == 0:
                o_ref[0, rows, :] = inter
            else:
                ob_s[rows, :] = inter
            sts[d] = dec_s[d, cc][0:1] * sts[d] + upd[n]
        return tuple(sts)

    lax.fori_loop(0, LAT_CHUNKS // SCAN_UNROLL, lat_body, ctx_states())
    worst = jnp.min(jnp.minimum(lower[0], lower[1]))
    bounded = worst >= 2.0 ** (-DIAG_LOG2_BOUND / SUB)

    @pl.when(bounded)
    def _():
        lax.fori_loop(0, LAT_CHUNKS // BOUNDED_GROUP, intra_body_bounded, 0)

    @pl.when(jnp.logical_not(bounded))
    def _():
        lax.fori_loop(0, LAT_CHUNKS // INTRA_GROUP, intra_body, 0)


def _hgrn2_call(z3, zc3, rec_lb, riders):
    lat = lambda col: pl.BlockSpec((1, SEQ, HEAD_W), lambda b, h: (b, 0, col + h))
    ctx = lambda col: pl.BlockSpec((1, CTX_LEN, HEAD_W), lambda b, h: (b, 0, col + h))
    assert len(riders) == N_RIDERS
    slab = lambda w: pl.BlockSpec((w.shape[0] // (BATCH * HEADS), w.shape[1]), lambda b, h: (b * HEADS + h, 0))
    return pl.pallas_call(
        _hgrn2_kernel,
        grid=(BATCH, HEADS),
        in_specs=[
            pl.BlockSpec((2, 2, HEAD_W), lambda b, h: (0, 0, h)),
            lat(COL_RFF), lat(COL_RFB), lat(COL_RI), lat(COL_RQ),
            ctx(COL_RFF), ctx(COL_RFB), ctx(COL_RI),
        ] + [slab(w) for w in riders],
        out_specs=[pl.BlockSpec((1, SEQ, HEAD_W), lambda b, h: (b, 0, h))] + [slab(w) for w in riders],
        out_shape=[jax.ShapeDtypeStruct((BATCH, SEQ, REC_WIDTH), F32)]
        + [jax.ShapeDtypeStruct(w.shape, BF16) for w in riders],
        scratch_shapes=[
            pltpu.VMEM((SEQ, HEAD_W), F32),
            pltpu.VMEM((2, SEQ, HEAD_W), F32),
            pltpu.VMEM((2, SEQ, HEAD_W), F32),
            pltpu.VMEM((2, SEQ, HEAD_W), BF16),
            pltpu.VMEM((2, SEQ, HEAD_W), BF16),
            pltpu.VMEM((2, LAT_CHUNKS, 8, HEAD_W), F32),
            pltpu.VMEM((2, CTX_LEN, HEAD_W), BF16),
            pltpu.VMEM((2, CTX_CHUNKS, 8, HEAD_W), F32),
            pltpu.VMEM((SEQ, HEAD_W), F32),
        ],
        compiler_params=_cparams(2, VMEM_LIMIT),
        name="hgrn2",
    )(rec_lb, z3, z3, z3, z3, zc3, zc3, zc3, *riders)


def _merge_kernel(att_ref, orec_ref, rg_ref, ga_ref, gr_ref, x_ref, g1_ref, sh2_ref, sc2_ref,
                  gnw_ref, n2w_ref, wba_ref, wbr_ref, wout_ref, x1_ref, h2_ref):
    rg = rg_ref[...]
    rec = _rms(orec_ref[...], gnw_ref[...]) * (rg * _sigmoid(rg))
    ya = jnp.dot(att_ref[...], wba_ref[...], preferred_element_type=F32)
    yr = jnp.dot(rec.astype(BF16), wbr_ref[...], preferred_element_type=F32)
    y = _sigmoid(ga_ref[...]) * ya + _sigmoid(gr_ref[...]) * yr
    x1 = x_ref[...] + g1_ref[0] * jnp.dot(y.astype(BF16), wout_ref[...], preferred_element_type=F32)
    x1_ref[...] = x1
    h2 = _rms(x1, n2w_ref[...]) * (1.0 + sc2_ref[0]) + sh2_ref[0]
    h2_ref[...] = h2.astype(BF16)


def _merge_call(att2d, orec2d, z2d, x2d, mod3, gnorm_w, norm2_w, wba, wbr, wout):
    tm = MERGE_TM
    m = x2d.shape[0]
    row = lambda i: (i, 0)
    const = lambda i: (0, 0)
    modspec = lambda k: pl.BlockSpec((1, 1, D_MODEL), lambda i: ((i * tm) // SEQ, 0, k))
    resident = lambda shape: pl.BlockSpec(shape, const, pipeline_mode=pl.Buffered(1))
    return pl.pallas_call(
        _merge_kernel,
        grid=(m // tm,),
        in_specs=[
            pl.BlockSpec((tm, ATTN_WIDTH), row),
            pl.BlockSpec((tm, REC_WIDTH), row),
            pl.BlockSpec((tm, REC_WIDTH), lambda i: (i, 7)),
            pl.BlockSpec((tm, D_MODEL), lambda i: (i, 4)),
            pl.BlockSpec((tm, D_MODEL), lambda i: (i, 5)),
            pl.BlockSpec((tm, D_MODEL), row),
            modspec(2), modspec(3), modspec(4),
            pl.BlockSpec((1, REC_WIDTH), const),
            pl.BlockSpec((1, D_MODEL), const),
            resident((ATTN_WIDTH, D_MODEL)),
            resident((REC_WIDTH, D_MODEL)),
            resident((D_MODEL, D_MODEL)),
        ],
        out_specs=[pl.BlockSpec((tm, D_MODEL), row), pl.BlockSpec((tm, D_MODEL), row)],
        out_shape=[jax.ShapeDtypeStruct((m, D_MODEL), F32), jax.ShapeDtypeStruct((m, D_MODEL), BF16)],
        compiler_params=_cparams(1, VMEM_LIMIT),
        name="merge",
    )(att2d, orec2d, z2d, z2d, z2d, x2d, mod3, mod3, mod3, gnorm_w, norm2_w, wba, wbr, wout)


FFN_PAD = 8


def _ffn_up_kernel(h_ref, wa_ref, wb_ref, cwa_ref, cwb_ref, cba_ref, cbb_ref, wd_ref, g_ref, wd16_ref, u_s):
    wd16_ref[...] = wd_ref[...].astype(BF16)
    h = h_ref[0]
    n_units = wa_ref.shape[1] // FFN_COLS
    units = []
    for c in range(n_units):
        cols = slice(c * FFN_COLS, (c + 1) * FFN_COLS)
        units.append((wa_ref, cwa_ref, cba_ref, cols))
        units.append((wb_ref, cwb_ref, cbb_ref, cols))

    pad = jnp.zeros((FFN_PAD, FFN_COLS), F32)
    for slot in range(2):
        u_s[slot, 0:FFN_PAD, :] = pad
        u_s[slot, FFN_PAD + SEQ:2 * FFN_PAD + SEQ, :] = pad

    def proj(n):
        w_ref, _, _, cols = units[n]
        u_s[n % 2, FFN_PAD:FFN_PAD + SEQ, :] = jnp.dot(h, w_ref[:, cols], preferred_element_type=F32)

    def conv(n):
        _, cw_ref, cb_ref, cols = units[n]
        cw = cw_ref[:, cols]
        buf = u_s.at[n % 2]
        return cb_ref[:, cols] + (buf[FFN_PAD - 1:FFN_PAD - 1 + SEQ, :] * cw[0:1]
                                  + buf[FFN_PAD:FFN_PAD + SEQ, :] * cw[1:2]
                                  + buf[FFN_PAD + 1:FFN_PAD + 1 + SEQ, :] * cw[2:3])

    proj(0)
    a = None
    for n in range(len(units)):
        if n + 1 < len(units):
            proj(n + 1)
        y = conv(n)
        if n % 2 == 0:
            a = y * _sigmoid(y)
        else:
            g_ref[0, :, units[n][3]] = (a * y).astype(BF16)


def _ffn_up_call(h3, w_up, conv_w, conv_b, w_down):
    tn = FFN_TN
    nb = FFN_DIM // tn
    wd_slab = pl.BlockSpec((FFN_DIM // (BATCH * nb), D_MODEL), lambda b, j: (b * nb + j, 0))
    return pl.pallas_call(
        _ffn_up_kernel,
        grid=(BATCH, nb),
        in_specs=[
            pl.BlockSpec((1, SEQ, D_MODEL), lambda b, j: (b, 0, 0)),
            pl.BlockSpec((D_MODEL, tn), lambda b, j: (0, j)),
            pl.BlockSpec((D_MODEL, tn), lambda b, j: (0, nb + j)),
            pl.BlockSpec((3, tn), lambda b, j: (0, j)),
            pl.BlockSpec((3, tn), lambda b, j: (0, nb + j)),
            pl.BlockSpec((1, tn), lambda b, j: (0, j)),
            pl.BlockSpec((1, tn), lambda b, j: (0, nb + j)),
            wd_slab,
        ],
        out_specs=[pl.BlockSpec((1, SEQ, tn), lambda b, j: (b, 0, j)), wd_slab],
        out_shape=[jax.ShapeDtypeStruct((BATCH, SEQ, FFN_DIM), BF16),
                   jax.ShapeDtypeStruct(w_down.shape, BF16)],
        scratch_shapes=[pltpu.VMEM((2, SEQ + 2 * FFN_PAD, FFN_COLS), F32)],
        compiler_params=_cparams(2, VMEM_LIMIT),
        name="ffn_up",
    )(h3, w_up, w_up, conv_w, conv_w, conv_b, conv_b, w_down)


def _ffn_down_kernel(g_ref, w_ref, x1_ref, g2_ref, fw_ref, o_ref):
    y = jnp.dot(g_ref[...], w_ref[...], preferred_element_type=F32)
    x2 = x1_ref[...] + g2_ref[0] * y
    o_ref[...] = _rms(x2, fw_ref[...])


def _ffn_down_call(g2d, w_down, x1, mod3, final_w):
    tm = FFN_DOWN_TM
    m = g2d.shape[0]
    return pl.pallas_call(
        _ffn_down_kernel,
        grid=(m // tm,),
        in_specs=[
            pl.BlockSpec((tm, FFN_DIM), lambda i: (i, 0)),
            pl.BlockSpec((FFN_DIM, D_MODEL), lambda i: (0, 0), pipeline_mode=pl.Buffered(1)),
            pl.BlockSpec((tm, D_MODEL), lambda i: (i, 0)),
            pl.BlockSpec((1, 1, D_MODEL), lambda i: ((i * tm) // SEQ, 0, 5)),
            pl.BlockSpec((1, D_MODEL), lambda i: (0, 0)),
        ],
        out_specs=pl.BlockSpec((tm, D_MODEL), lambda i: (i, 0)),
        out_shape=jax.ShapeDtypeStruct((m, D_MODEL), F32),
        compiler_params=_cparams(1, VMEM_LIMIT),
        name="ffn_down",
    )(g2d, w_down, x1, mod3, final_w)


def _rope_tables():
    rows = SEQ // GRID_W
    r, col = jnp.meshgrid(jnp.arange(rows), jnp.arange(GRID_W), indexing="ij")
    pos = jnp.stack([r.reshape(-1), col.reshape(-1)], axis=-1).astype(F32)
    nq = HEAD_DIM // 4
    inv = ROPE_BASE ** (-jnp.arange(nq, dtype=F32) / nq)
    ang = pos[:, :, None] * inv
    cos, sin = jnp.cos(ang), jnp.sin(ang)
    lane = jnp.arange(HEAD_W)
    axis = (lane % HEAD_DIM) // (2 * nq)
    second = ((lane % (2 * nq)) // nq) == 1
    freq = lane % nq
    c_t = cos[:, axis, freq]
    s_t = sin[:, axis, freq]
    sa_t = jnp.where(second[None, :], 0.0, -s_t)
    sb_t = jnp.where(second[None, :], s_t, 0.0)
    return c_t, sa_t, sb_t


def kernel(x, c, ctx, c_ctx, w_mod, b_mod, norm1_w, w_in, lam_q1, lam_k1, lam_q2, lam_k2, subln_w,
           rec_lb, rec_gnorm_w, w_branch_attn, w_branch_rec, w_out, norm2_w, w_up, conv_w, conv_b,
           w_down, final_norm_w):
    m_lat = BATCH * SEQ
    c_all = jnp.concatenate([c, c_ctx[None, :], jnp.zeros((MOD_ROWS - BATCH - 1, D_MODEL), F32)], axis=0)
    mod = _mod_call(c_all, w_mod[0], b_mod[0][None, :])
    mod3 = mod.reshape(MOD_ROWS, 1, N_MOD * D_MODEL)

    w_in16 = w_in[0].astype(BF16)
    n1w = norm1_w[0][None, :]
    x2d = x.reshape(m_lat, D_MODEL)
    z = _inproj_call(x2d, n1w, mod3, w_in16, IN_WIDTH, lambda i: (i * INPROJ_TM) // SEQ, INPROJ_TM, INPROJ_TN,
                     "inproj")
    zc = _inproj_call(ctx.reshape(BATCH * CTX_LEN, D_MODEL), n1w, mod3, w_in16, CTX_KV_WIDTH,
                      lambda i: CTX_MOD_ROW, INPROJ_TM, INPROJ_CTX_TN, "inproj_ctx")
    z3 = z.reshape(BATCH, SEQ, IN_WIDTH)
    zc3 = zc.reshape(BATCH, CTX_LEN, CTX_KV_WIDTH)

    lam_params = jnp.stack([lam_q1[0], lam_k1[0], lam_q2[0], lam_k2[0]], axis=0)
    att = _attn_call(z3, zc3, lam_params, _rope_tables(), subln_w[0][None, :])
    orec, w_up16, w_out16, w_ba16, w_br16 = _hgrn2_call(
        z3, zc3, rec_lb, [w_up[0], w_out[0], w_branch_attn[0], w_branch_rec[0]])

    x1, h2 = _merge_call(att.reshape(m_lat, ATTN_WIDTH), orec.reshape(m_lat, REC_WIDTH), z, x2d, mod3,
                         rec_gnorm_w[0][None, :], norm2_w[0][None, :],
                         w_ba16, w_br16, w_out16)
    g, w_down16 = _ffn_up_call(h2.reshape(BATCH, SEQ, D_MODEL), w_up16, conv_w[0], conv_b[0][None, :], w_down[0])
    out = _ffn_down_call(g.reshape(m_lat, FFN_DIM), w_down16, x1, mod3, final_norm_w[None, :])
    return out.reshape(BATCH, SEQ, D_MODEL)
```

```python
import math

import jax
import jax.numpy as jnp
from jax import lax
from jax.experimental import pallas as pl
from jax.experimental.pallas import tpu as pltpu

F32 = jnp.float32
BF16 = jnp.bfloat16

D_MODEL = 2048
BATCH = 8
SEQ = 2048
GRID_W = 64
CTX_LEN = 256
EPS = 1e-6
N_MOD = 6
HEADS = 8
HEAD_DIM = 64
HEAD_W = 128
ATTN_WIDTH = 1024
REC_WIDTH = 1024
FFN_DIM = 5632
IN_WIDTH = 12288
CTX_KV_WIDTH = 5120
ROPE_BASE = 10000.0
LAM_INIT = 0.8 - 0.6 * math.exp(-0.3 * 0)

COL_AK, COL_AV, COL_RFF, COL_RFB, COL_RI, COL_AQ, COL_RQ = 0, 8, 16, 24, 32, 40, 48
MOD_ROWS = 16
CTX_MOD_ROW = BATCH

CHUNK = 64
SUB = 16
LOG2E = 1.4426950408889634

VMEM_LIMIT = 56 * 1024 * 1024
MOD_TN = 2048
INPROJ_TM = 1024
INPROJ_TN, INPROJ_CTX_TN = 1536, 1280
ATTN_SUB = 512
MERGE_TM = 256
FFN_TN = 512
FFN_COLS = 256
FFN_DOWN_TM = 512


def _cparams(n_axes, vmem=None):
    return pltpu.CompilerParams(
        dimension_semantics=("arbitrary",) * n_axes,
        vmem_limit_bytes=vmem,
    )


def _sigmoid(x):
    return 1.0 / (1.0 + jnp.exp2(x * (-LOG2E)))


def _rms(x, w):
    return x * lax.rsqrt(jnp.mean(x * x, axis=-1, keepdims=True) + EPS) * w


def _mod_kernel(c_ref, w_ref, b_ref, o_ref):
    c = c_ref[...]
    a = (c * _sigmoid(c)).astype(BF16)
    o_ref[...] = jnp.dot(a, w_ref[...].astype(BF16), preferred_element_type=F32) + b_ref[...]


def _mod_call(c_all, w_mod, b_mod):
    tn = MOD_TN
    n = w_mod.shape[1]
    return pl.pallas_call(
        _mod_kernel,
        grid=(n // tn,),
        in_specs=[
            pl.BlockSpec((MOD_ROWS, D_MODEL), lambda j: (0, 0)),
            pl.BlockSpec((D_MODEL, tn), lambda j: (0, j)),
            pl.BlockSpec((1, tn), lambda j: (0, j)),
        ],
        out_specs=pl.BlockSpec((MOD_ROWS, tn), lambda j: (0, j)),
        out_shape=jax.ShapeDtypeStruct((MOD_ROWS, n), F32),
        compiler_params=_cparams(1, VMEM_LIMIT),
        name="mod",
    )(c_all, w_mod, b_mod)


def _inproj_kernel(x_ref, nw_ref, sh_ref, sc_ref, w_ref, o_ref, h_ref):
    @pl.when(pl.program_id(1) == 0)
    def _():
        h = _rms(x_ref[...], nw_ref[...]) * (1.0 + sc_ref[0]) + sh_ref[0]
        h_ref[...] = h.astype(BF16)

    o_ref[...] = jnp.dot(h_ref[...], w_ref[...], preferred_element_type=F32)


def _inproj_call(x2d, norm_w, mod3, w_bf16, n_out, mod_row_of_tile, tm, tn, name):
    m = x2d.shape[0]
    return pl.pallas_call(
        _inproj_kernel,
        grid=(m // tm, n_out // tn),
        in_specs=[
            pl.BlockSpec((tm, D_MODEL), lambda i, j: (i, 0)),
            pl.BlockSpec((1, D_MODEL), lambda i, j: (0, 0)),
            pl.BlockSpec((1, 1, D_MODEL), lambda i, j: (mod_row_of_tile(i), 0, 0)),
            pl.BlockSpec((1, 1, D_MODEL), lambda i, j: (mod_row_of_tile(i), 0, 1)),
            pl.BlockSpec((D_MODEL, tn), lambda i, j: (0, j)),
        ],
        out_specs=pl.BlockSpec((tm, tn), lambda i, j: (i, j)),
        out_shape=jax.ShapeDtypeStruct((m, n_out), F32),
        scratch_shapes=[pltpu.VMEM((tm, D_MODEL), BF16)],
        compiler_params=_cparams(2, VMEM_LIMIT),
        name=name,
    )(x2d, norm_w, mod3, mod3, w_bf16)


def _rope(x, c, sa, sb):
    return x * c + pltpu.roll(x, HEAD_W - 16, 1) * sa + pltpu.roll(x, 16, 1) * sb


def _attn_kernel(lamp_ref, q_ref, k_ref, v_ref, kc_ref, vc_ref, c_ref, sa_ref, sb_ref, subln_ref, o_ref, k_s, v_s):
    n_sub = SEQ // ATTN_SUB
    c, sa, sb = c_ref[...], sa_ref[...], sb_ref[...]

    k_s[0:SEQ, :] = _rope(k_ref[0], c, sa, sb).astype(BF16)
    k_s[SEQ:SEQ + CTX_LEN, :] = kc_ref[0].astype(BF16)
    v_s[0:SEQ, 0:HEAD_W] = v_ref[0].astype(BF16)
    v_s[SEQ:SEQ + CTX_LEN, 0:HEAD_W] = vc_ref[0].astype(BF16)
    v_s[:, HEAD_W:2 * HEAD_W] = jnp.ones((SEQ + CTX_LEN, HEAD_W), BF16)

    lp = lamp_ref[...]
    lam = (jnp.exp(jnp.sum(lp[0:1] * lp[1:2], axis=-1, keepdims=True))
           - jnp.exp(jnp.sum(lp[2:3] * lp[3:4], axis=-1, keepdims=True)) + LAM_INIT)

    q = _rope(q_ref[0], c, sa, sb) * (HEAD_DIM ** -0.5 * LOG2E)
    lane = lax.broadcasted_iota(jnp.int32, q.shape, 1)
    q0 = jnp.where(lane < HEAD_DIM, q, 0.0).astype(BF16)
    q1 = jnp.where(lane >= HEAD_DIM, q, 0.0).astype(BF16)

    def scores(i):
        lo, hi = i * ATTN_SUB, (i + 1) * ATTN_SUB
        qq = jnp.concatenate([q0[lo:hi], q1[lo:hi]], axis=0)
        return lax.dot_general(qq, k_s[...], (((1,), (1,)), ((), ())), preferred_element_type=F32)

    s_next = scores(0)
    for i in range(n_sub):
        s = s_next
        if i + 1 < n_sub:
            s_next = scores(i + 1)
        e = jnp.exp2(s - jnp.max(s, axis=-1, keepdims=True)).astype(BF16)
        r = jnp.dot(e, v_s[...], preferred_element_type=F32)
        p = r[:, 0:HEAD_W] / r[:, HEAD_W:2 * HEAD_W]
        o = p[0:ATTN_SUB] - lam * p[ATTN_SUB:2 * ATTN_SUB]
        o = _rms(o, subln_ref[...]) * (1.0 - LAM_INIT)
        o_ref[0, i * ATTN_SUB:(i + 1) * ATTN_SUB, :] = o.astype(BF16)


def _attn_call(z3, zc3, lam_params, rope_tabs, subln_w):
    lat = lambda col: pl.BlockSpec((1, SEQ, HEAD_W), lambda b, h: (b, 0, col + h))
    ctx = lambda col: pl.BlockSpec((1, CTX_LEN, HEAD_W), lambda b, h: (b, 0, col + h))
    full = lambda shape: pl.BlockSpec(shape, lambda b, h: (0, 0))
    return pl.pallas_call(
        _attn_kernel,
        grid=(BATCH, HEADS),
        in_specs=[full((4, HEAD_DIM)), lat(COL_AQ), lat(COL_AK), lat(COL_AV), ctx(COL_AK), ctx(COL_AV),
                  full((SEQ, HEAD_W)), full((SEQ, HEAD_W)), full((SEQ, HEAD_W)), full((1, HEAD_W))],
        out_specs=pl.BlockSpec((1, SEQ, HEAD_W), lambda b, h: (b, 0, h)),
        out_shape=jax.ShapeDtypeStruct((BATCH, SEQ, ATTN_WIDTH), BF16),
        scratch_shapes=[pltpu.VMEM((SEQ + CTX_LEN, HEAD_W), BF16),
                        pltpu.VMEM((SEQ + CTX_LEN, 2 * HEAD_W), BF16)],
        compiler_params=_cparams(2, VMEM_LIMIT),
        name="attn",
    )(lam_params, z3, z3, z3, zc3, zc3, *rope_tabs, subln_w)


LAT_CHUNKS = SEQ // CHUNK
CTX_CHUNKS = CTX_LEN // CHUNK
PREP_ROWS = 1024
INTRA_GROUP = 2
BOUNDED_GROUP = 32
DIAG_LOG2_BOUND = 96.0
SCAN_UNROLL = 32


def _tri(rev):
    r = lax.broadcasted_iota(jnp.int32, (CHUNK, CHUNK), 0)
    c = lax.broadcasted_iota(jnp.int32, (CHUNK, CHUNK), 1)
    return jnp.where((c >= r) if rev else (c <= r), 1.0, 0.0).astype(F32)


def _gate(f_raw, lower):
    f = lower + (1.0 - lower) * _sigmoid(f_raw)
    return 1.0 - f, jnp.log2(f)


def _prep(f_raw, lower, rev):
    n = f_raw.shape[0] // CHUNK
    k, logf = _gate(f_raw, lower)
    wide = jnp.concatenate([logf[i * CHUNK:(i + 1) * CHUNK] for i in range(n)], axis=1)
    tri = _tri(rev).astype(BF16)
    hi = wide.astype(BF16)
    rest = wide - hi.astype(F32)
    mid = rest.astype(BF16)
    lo = (rest - mid.astype(F32)).astype(BF16)
    bw = (jnp.dot(tri, hi, preferred_element_type=F32) + jnp.dot(tri, mid, preferred_element_type=F32)
          + jnp.dot(tri, lo, preferred_element_type=F32))
    t = 0 if rev else CHUNK - 1
    b = jnp.concatenate([bw[:, i * HEAD_W:(i + 1) * HEAD_W] for i in range(n)], axis=0)
    tots = [bw[t:t + 1, i * HEAD_W:(i + 1) * HEAD_W] for i in range(n)]
    b_tot = jnp.concatenate([jnp.broadcast_to(r, (CHUNK, HEAD_W)) for r in tots], axis=0)
    return k, b, b_tot, tots


def _diag_block(q_s, k_d, b_d, v_ref, base, rev):
    rows = lax.broadcasted_iota(jnp.int32, (8, HEAD_W), 0)
    halves = range(SUB // 8)
    qh = [q_s[pl.ds(base + 8 * u, 8), :] for u in halves]
    bh = [b_d[pl.ds(base + 8 * u, 8), :] for u in halves]
    acc = [jnp.zeros((8, HEAD_W), F32) for _ in halves]
    for s in range(SUB):
        bs = b_d[pl.ds(base + s, 1), :]
        ks = k_d[pl.ds(base + s, 1), :]
        vs = v_ref[0, pl.ds(base + s, 1), :]
        for u in halves:
            t_lo, t_hi = 8 * u, 8 * u + 7
            if (t_lo > s) if rev else (t_hi < s):
                continue
            w = jnp.exp2(bh[u] - bs)
            if (t_hi > s) if rev else (t_lo < s):
                keep = (rows + t_lo <= s) if rev else (rows + t_lo >= s)
                w = jnp.where(keep, w, 0.0)
            col = jnp.sum(qh[u] * w * ks, axis=-1, keepdims=True)
            acc[u] = acc[u] + col * vs
    return jnp.concatenate(acc, axis=0)


def _block_scores(q, k, b, rev, with_diag):
    blocks = []
    for i in range(CHUNK // SUB):
        lo, hi = SUB * i, SUB * (i + 1)
        if rev:
            e_lo, e_hi = (lo if with_diag else hi), CHUNK
            ref_row = b[hi:hi + 1] if hi < CHUNK else None
        else:
            e_lo, e_hi = 0, (hi if with_diag else lo)
            ref_row = b[lo - 1:lo] if lo > 0 else None
        if ref_row is None:
            if not with_diag:
                blocks.append(jnp.zeros((SUB, CHUNK), F32))
                continue
            ref_row = jnp.zeros((1, HEAD_W), F32)
        qt = (q[lo:hi] * jnp.exp2(b[lo:hi] - ref_row)).astype(BF16)
        kt = (k[e_lo:e_hi] * jnp.exp2(ref_row - b[e_lo:e_hi])).astype(BF16)
        parts = []
        if e_lo > 0:
            parts.append(jnp.zeros((e_lo, HEAD_W), BF16))
        parts.append(kt)
        if e_hi < CHUNK:
            parts.append(jnp.zeros((CHUNK - e_hi, HEAD_W), BF16))
        ktp = jnp.concatenate(parts, axis=0)
        blocks.append(lax.dot_general(qt, ktp, (((1,), (1,)), ((), ())), preferred_element_type=F32))
    return jnp.concatenate(blocks, axis=0)


N_RIDERS = 4


def _hgrn2_kernel(lb_ref, ff_ref, fb_ref, v_ref, q_ref, ffc_ref, fbc_ref, vc_ref, *refs):
    cast_in, (o_ref, *cast_out) = refs[:N_RIDERS], refs[N_RIDERS:2 * N_RIDERS + 1]
    q_s, k_s, b_s, qbar_s, kbar_s, dec_s, kbarc_s, decc_s, ob_s = refs[2 * N_RIDERS + 1:]
    for w_f32, w_bf16 in zip(cast_in, cast_out):
        w_bf16[...] = w_f32[...].astype(BF16)

    lb = lb_ref[...]
    lower = []
    for d in range(2):
        l0, l1 = lb[d, 0:1], lb[d, 1:2]
        mx = jnp.maximum(l0, l1)
        e0, e1 = jnp.exp(l0 - mx), jnp.exp(l1 - mx)
        lower.append(e0 / (e0 + e1))
    f_lat = (ff_ref, fb_ref)
    f_ctx = (ffc_ref, fbc_ref)

    for d in range(2):
        k, b, b_tot, tots = _prep(f_ctx[d][0], lower[d], d == 1)
        kbarc_s[d] = (k * jnp.exp2(b_tot - b)).astype(BF16)
        for c in range(CTX_CHUNKS):
            decc_s[d, c] = jnp.broadcast_to(jnp.exp2(tots[c]), (8, HEAD_W))

    def prep_body(g, carry):
        rows = pl.ds(pl.multiple_of(g * PREP_ROWS, PREP_ROWS), PREP_ROWS)
        qr = q_ref[0, rows, :]
        q = qr * _sigmoid(qr)
        q_s[rows, :] = q
        for d in range(2):
            k, b, b_tot, tots = _prep(f_lat[d][0, rows, :], lower[d], d == 1)
            k_s[d, rows, :] = k
            b_s[d, rows, :] = b
            qbar_s[d, rows, :] = (q * jnp.exp2(b)).astype(BF16)
            kbar_s[d, rows, :] = (k * jnp.exp2(b_tot - b)).astype(BF16)
            for i in range(PREP_ROWS // CHUNK):
                dec_s[d, g * (PREP_ROWS // CHUNK) + i] = jnp.broadcast_to(jnp.exp2(tots[i]), (8, HEAD_W))
        return carry

    lax.fori_loop(0, SEQ // PREP_ROWS, prep_body, 0)

    def chunk_scores(base, with_diag):
        rows = pl.ds(base, CHUNK)
        out = []
        for d in range(2):
            a = _block_scores(q_s[rows, :], k_s[d, rows, :], b_s[d, rows, :], d == 1, with_diag)
            if with_diag:
                r = lax.broadcasted_iota(jnp.int32, a.shape, 0)
                c = lax.broadcasted_iota(jnp.int32, a.shape, 1)
                a = jnp.where((c >= r) if d == 1 else (c <= r), a, 0.0)
            out.append(a.astype(BF16))
        return out

    def value_products(bases, scores):
        pv = []
        for j, base in enumerate(bases):
            v16 = v_ref[0, pl.ds(base, CHUNK), :].astype(BF16)
            pv.append(jnp.dot(scores[j][0], v16, preferred_element_type=F32)
                      + jnp.dot(scores[j][1], v16, preferred_element_type=F32))
        return pv

    def intra_body(g, carry):
        bases = [pl.multiple_of((g * INTRA_GROUP + j) * CHUNK, CHUNK) for j in range(INTRA_GROUP)]

        def diag(base):
            blocks = []
            for i in range(CHUNK // SUB):
                blk = pl.multiple_of(base + SUB * i, SUB)
                blocks.append(_diag_block(q_s, k_s.at[0], b_s.at[0], v_ref, blk, False)
                              + _diag_block(q_s, k_s.at[1], b_s.at[1], v_ref, blk, True))
            return jnp.concatenate(blocks, axis=0)

        scores = [chunk_scores(base, False) for base in bases]
        od = [diag(bases[0])]
        pv = value_products(bases, scores)
        od += [diag(base) for base in bases[1:]]
        for j, base in enumerate(bases):
            rows = pl.ds(base, CHUNK)
            o_ref[0, rows, :] = (o_ref[0, rows, :] + ob_s[rows, :]) + (pv[j] + od[j])
        return carry

    def intra_body_bounded(g, carry):
        bases = [pl.multiple_of((g * BOUNDED_GROUP + j) * CHUNK, CHUNK) for j in range(BOUNDED_GROUP)]
        pv = value_products(bases, [chunk_scores(base, True) for base in bases])
        for j, base in enumerate(bases):
            rows = pl.ds(base, CHUNK)
            o_ref[0, rows, :] = (o_ref[0, rows, :] + ob_s[rows, :]) + pv[j]
        return carry

    def ctx_states():
        steps = [(d, (CTX_CHUNKS - 1 - c) if d == 1 else c) for c in range(CTX_CHUNKS) for d in range(2)]
        upd = [lax.dot_general(vc_ref[0, cc * CHUNK:(cc + 1) * CHUNK, :].astype(BF16),
                               kbarc_s[d, cc * CHUNK:(cc + 1) * CHUNK, :], (((0,), (0,)), ((), ())),
                               preferred_element_type=F32) for d, cc in steps]
        sts = [None, None]
        for n, (d, cc) in enumerate(steps):
            sts[d] = upd[n] if sts[d] is None else decc_s[d, cc][0:1] * sts[d] + upd[n]
        return tuple(sts)

    def lat_body(it, sts):
        sts = list(sts)
        steps = []
        for u in range(SCAN_UNROLL):
            c = it * SCAN_UNROLL + u
            for d in range(2):
                cc = (LAT_CHUNKS - 1 - c) if d == 1 else c
                steps.append((d, cc, pl.ds(pl.multiple_of(cc * CHUNK, CHUNK), CHUNK)))
        upd = [lax.dot_general(v_ref[0, rows, :].astype(BF16), kbar_s[d, rows, :], (((0,), (0,)), ((), ())),
                               preferred_element_type=F32) for d, cc, rows in steps]
        for n, (d, cc, rows) in enumerate(steps):
            inter = lax.dot_general(qbar_s[d, rows, :], sts[d].astype(BF16), (((1,), (1,)), ((), ())),
                                    preferred_element_type=F32)
            if d == 0:
                o_ref[0, rows, :] = inter
            else:
                ob_s[rows, :] = inter
            sts[d] = dec_s[d, cc][0:1] * sts[d] + upd[n]
        return tuple(sts)

    lax.fori_loop(0, LAT_CHUNKS // SCAN_UNROLL, lat_body, ctx_states())
    worst = jnp.min(jnp.minimum(lower[0], lower[1]))
    bounded = worst >= 2.0 ** (-DIAG_LOG2_BOUND / SUB)

    @pl.when(bounded)
    def _():
        lax.fori_loop(0, LAT_CHUNKS // BOUNDED_GROUP, intra_body_bounded, 0)

    @pl.when(jnp.logical_not(bounded))
    def _():
        lax.fori_loop(0, LAT_CHUNKS // INTRA_GROUP, intra_body, 0)


def _hgrn2_call(z3, zc3, rec_lb, riders):
    lat = lambda col: pl.BlockSpec((1, SEQ, HEAD_W), lambda b, h: (b, 0, col + h))
    ctx = lambda col: pl.BlockSpec((1, CTX_LEN, HEAD_W), lambda b, h: (b, 0, col + h))
    assert len(riders) == N_RIDERS
    slab = lambda w: pl.BlockSpec((w.shape[0] // (BATCH * HEADS), w.shape[1]), lambda b, h: (b * HEADS + h, 0))
    return pl.pallas_call(
        _hgrn2_kernel,
        grid=(BATCH, HEADS),
        in_specs=[
            pl.BlockSpec((2, 2, HEAD_W), lambda b, h: (0, 0, h)),
            lat(COL_RFF), lat(COL_RFB), lat(COL_RI), lat(COL_RQ),
            ctx(COL_RFF), ctx(COL_RFB), ctx(COL_RI),
        ] + [slab(w) for w in riders],
        out_specs=[pl.BlockSpec((1, SEQ, HEAD_W), lambda b, h: (b, 0, h))] + [slab(w) for w in riders],
        out_shape=[jax.ShapeDtypeStruct((BATCH, SEQ, REC_WIDTH), F32)]
        + [jax.ShapeDtypeStruct(w.shape, BF16) for w in riders],
        scratch_shapes=[
            pltpu.VMEM((SEQ, HEAD_W), F32),
            pltpu.VMEM((2, SEQ, HEAD_W), F32),
            pltpu.VMEM((2, SEQ, HEAD_W), F32),
            pltpu.VMEM((2, SEQ, HEAD_W), BF16),
            pltpu.VMEM((2, SEQ, HEAD_W), BF16),
            pltpu.VMEM((2, LAT_CHUNKS, 8, HEAD_W), F32),
            pltpu.VMEM((2, CTX_LEN, HEAD_W), BF16),
            pltpu.VMEM((2, CTX_CHUNKS, 8, HEAD_W), F32),
            pltpu.VMEM((SEQ, HEAD_W), F32),
        ],
        compiler_params=_cparams(2, VMEM_LIMIT),
        name="hgrn2",
    )(rec_lb, z3, z3, z3, z3, zc3, zc3, zc3, *riders)


def _merge_kernel(att_ref, orec_ref, rg_ref, ga_ref, gr_ref, x_ref, g1_ref, sh2_ref, sc2_ref,
                  gnw_ref, n2w_ref, wba_ref, wbr_ref, wout_ref, x1_ref, h2_ref):
    rg = rg_ref[...]
    rec = _rms(orec_ref[...], gnw_ref[...]) * (rg * _sigmoid(rg))
    ya = jnp.dot(att_ref[...], wba_ref[...], preferred_element_type=F32)
    yr = jnp.dot(rec.astype(BF16), wbr_ref[...], preferred_element_type=F32)
    y = _sigmoid(ga_ref[...]) * ya + _sigmoid(gr_ref[...]) * yr
    x1 = x_ref[...] + g1_ref[0] * jnp.dot(y.astype(BF16), wout_ref[...], preferred_element_type=F32)
    x1_ref[...] = x1
    h2 = _rms(x1, n2w_ref[...]) * (1.0 + sc2_ref[0]) + sh2_ref[0]
    h2_ref[...] = h2.astype(BF16)


def _merge_call(att2d, orec2d, z2d, x2d, mod3, gnorm_w, norm2_w, wba, wbr, wout):
    tm = MERGE_TM
    m = x2d.shape[0]
    row = lambda i: (i, 0)
    const = lambda i: (0, 0)
    modspec = lambda k: pl.BlockSpec((1, 1, D_MODEL), lambda i: ((i * tm) // SEQ, 0, k))
    resident = lambda shape: pl.BlockSpec(shape, const, pipeline_mode=pl.Buffered(1))
    return pl.pallas_call(
        _merge_kernel,
        grid=(m // tm,),
        in_specs=[
            pl.BlockSpec((tm, ATTN_WIDTH), row),
            pl.BlockSpec((tm, REC_WIDTH), row),
            pl.BlockSpec((tm, REC_WIDTH), lambda i: (i, 7)),
            pl.BlockSpec((tm, D_MODEL), lambda i: (i, 4)),
            pl.BlockSpec((tm, D_MODEL), lambda i: (i, 5)),
            pl.BlockSpec((tm, D_MODEL), row),
            modspec(2), modspec(3), modspec(4),
            pl.BlockSpec((1, REC_WIDTH), const),
            pl.BlockSpec((1, D_MODEL), const),
            resident((ATTN_WIDTH, D_MODEL)),
            resident((REC_WIDTH, D_MODEL)),
            resident((D_MODEL, D_MODEL)),
        ],
        out_specs=[pl.BlockSpec((tm, D_MODEL), row), pl.BlockSpec((tm, D_MODEL), row)],
        out_shape=[jax.ShapeDtypeStruct((m, D_MODEL), F32), jax.ShapeDtypeStruct((m, D_MODEL), BF16)],
        compiler_params=_cparams(1, VMEM_LIMIT),
        name="merge",
    )(att2d, orec2d, z2d, z2d, z2d, x2d, mod3, mod3, mod3, gnorm_w, norm2_w, wba, wbr, wout)


FFN_PAD = 8


def _ffn_up_kernel(h_ref, wa_ref, wb_ref, cwa_ref, cwb_ref, cba_ref, cbb_ref, wd_ref, g_ref, wd16_ref, u_s):
    wd16_ref[...] = wd_ref[...].astype(BF16)
    h = h_ref[0]
    n_units = wa_ref.shape[1] // FFN_COLS
    units = []
    for c in range(n_units):
        cols = slice(c * FFN_COLS, (c + 1) * FFN_COLS)
        units.append((wa_ref, cwa_ref, cba_ref, cols))
        units.append((wb_ref, cwb_ref, cbb_ref, cols))

    pad = jnp.zeros((FFN_PAD, FFN_COLS), F32)
    for slot in range(2):
        u_s[slot, 0:FFN_PAD, :] = pad
        u_s[slot, FFN_PAD + SEQ:2 * FFN_PAD + SEQ, :] = pad

    def proj(n):
        w_ref, _, _, cols = units[n]
        u_s[n % 2, FFN_PAD:FFN_PAD + SEQ, :] = jnp.dot(h, w_ref[:, cols], preferred_element_type=F32)

    def conv(n):
        _, cw_ref, cb_ref, cols = units[n]
        cw = cw_ref[:, cols]
        buf = u_s.at[n % 2]
        return cb_ref[:, cols] + (buf[FFN_PAD - 1:FFN_PAD - 1 + SEQ, :] * cw[0:1]
                                  + buf[FFN_PAD:FFN_PAD + SEQ, :] * cw[1:2]
                                  + buf[FFN_PAD + 1:FFN_PAD + 1 + SEQ, :] * cw[2:3])

    proj(0)
    a = None
    for n in range(len(units)):
        if n + 1 < len(units):
            proj(n + 1)
        y = conv(n)
        if n % 2 == 0:
            a = y * _sigmoid(y)
        else:
            g_ref[0, :, units[n][3]] = (a * y).astype(BF16)


def _ffn_up_call(h3, w_up, conv_w, conv_b, w_down):
    tn = FFN_TN
    nb = FFN_DIM // tn
    wd_slab = pl.BlockSpec((FFN_DIM // (BATCH * nb), D_MODEL), lambda b, j: (b * nb + j, 0))
    return pl.pallas_call(
        _ffn_up_kernel,
        grid=(BATCH, nb),
        in_specs=[
            pl.BlockSpec((1, SEQ, D_MODEL), lambda b, j: (b, 0, 0)),
            pl.BlockSpec((D_MODEL, tn), lambda b, j: (0, j)),
            pl.BlockSpec((D_MODEL, tn), lambda b, j: (0, nb + j)),
            pl.BlockSpec((3, tn), lambda b, j: (0, j)),
            pl.BlockSpec((3, tn), lambda b, j: (0, nb + j)),
            pl.BlockSpec((1, tn), lambda b, j: (0, j)),
            pl.BlockSpec((1, tn), lambda b, j: (0, nb + j)),
            wd_slab,
        ],
        out_specs=[pl.BlockSpec((1, SEQ, tn), lambda b, j: (b, 0, j)), wd_slab],
        out_shape=[jax.ShapeDtypeStruct((BATCH, SEQ, FFN_DIM), BF16),
                   jax.ShapeDtypeStruct(w_down.shape, BF16)],
        scratch_shapes=[pltpu.VMEM((2, SEQ + 2 * FFN_PAD, FFN_COLS), F32)],
        compiler_params=_cparams(2, VMEM_LIMIT),
        name="ffn_up",
    )(h3, w_up, w_up, conv_w, conv_w, conv_b, conv_b, w_down)


def _ffn_down_kernel(g_ref, w_ref, x1_ref, g2_ref, fw_ref, o_ref):
    y = jnp.dot(g_ref[...], w_ref[...], preferred_element_type=F32)
    x2 = x1_ref[...] + g2_ref[0] * y
    o_ref[...] = _rms(x2, fw_ref[...])


def _ffn_down_call(g2d, w_down, x1, mod3, final_w):
    tm = FFN_DOWN_TM
    m = g2d.shape[0]
    return pl.pallas_call(
        _ffn_down_kernel,
        grid=(m // tm,),
        in_specs=[
            pl.BlockSpec((tm, FFN_DIM), lambda i: (i, 0)),
            pl.BlockSpec((FFN_DIM, D_MODEL), lambda i: (0, 0), pipeline_mode=pl.Buffered(1)),
            pl.BlockSpec((tm, D_MODEL), lambda i: (i, 0)),
            pl.BlockSpec((1, 1, D_MODEL), lambda i: ((i * tm) // SEQ, 0, 5)),
            pl.BlockSpec((1, D_MODEL), lambda i: (0, 0)),
        ],
        out_specs=pl.BlockSpec((tm, D_MODEL), lambda i: (i, 0)),
        out_shape=jax.ShapeDtypeStruct((m, D_MODEL), F32),
        compiler_params=_cparams(1, VMEM_LIMIT),
        name="ffn_down",
    )(g2d, w_down, x1, mod3, final_w)


def _rope_tables():
    rows = SEQ // GRID_W
    r, col = jnp.meshgrid(jnp.arange(rows), jnp.arange(GRID_W), indexing="ij")
    pos = jnp.stack([r.reshape(-1), col.reshape(-1)], axis=-1).astype(F32)
    nq = HEAD_DIM // 4
    inv = ROPE_BASE ** (-jnp.arange(nq, dtype=F32) / nq)
    ang = pos[:, :, None] * inv
    cos, sin = jnp.cos(ang), jnp.sin(ang)
    lane = jnp.arange(HEAD_W)
    axis = (lane % HEAD_DIM) // (2 * nq)
    second = ((lane % (2 * nq)) // nq) == 1
    freq = lane % nq
    c_t = cos[:, axis, freq]
    s_t = sin[:, axis, freq]
    sa_t = jnp.where(second[None, :], 0.0, -s_t)
    sb_t = jnp.where(second[None, :], s_t, 0.0)
    return c_t, sa_t, sb_t


def kernel(x, c, ctx, c_ctx, w_mod, b_mod, norm1_w, w_in, lam_q1, lam_k1, lam_q2, lam_k2, subln_w,
           rec_lb, rec_gnorm_w, w_branch_attn, w_branch_rec, w_out, norm2_w, w_up, conv_w, conv_b,
           w_down, final_norm_w):
    m_lat = BATCH * SEQ
    c_all = jnp.concatenate([c, c_ctx[None, :], jnp.zeros((MOD_ROWS - BATCH - 1, D_MODEL), F32)], axis=0)
    mod = _mod_call(c_all, w_mod[0], b_mod[0][None, :])
    mod3 = mod.reshape(MOD_ROWS, 1, N_MOD * D_MODEL)

    w_in16 = w_in[0].astype(BF16)
    n1w = norm1_w[0][None, :]
    x2d = x.reshape(m_lat, D_MODEL)
    z = _inproj_call(x2d, n1w, mod3, w_in16, IN_WIDTH, lambda i: (i * INPROJ_TM) // SEQ, INPROJ_TM, INPROJ_TN,
                     "inproj")
    zc = _inproj_call(ctx.reshape(BATCH * CTX_LEN, D_MODEL), n1w, mod3, w_in16, CTX_KV_WIDTH,
                      lambda i: CTX_MOD_ROW, INPROJ_TM, INPROJ_CTX_TN, "inproj_ctx")
    z3 = z.reshape(BATCH, SEQ, IN_WIDTH)
    zc3 = zc.reshape(BATCH, CTX_LEN, CTX_KV_WIDTH)

    lam_params = jnp.stack([lam_q1[0], lam_k1[0], lam_q2[0], lam_k2[0]], axis=0)
    att = _attn_call(z3, zc3, lam_params, _rope_tables(), subln_w[0][None, :])
    orec, w_up16, w_out16, w_ba16, w_br16 = _hgrn2_call(
        z3, zc3, rec_lb, [w_up[0], w_out[0], w_branch_attn[0], w_branch_rec[0]])

    x1, h2 = _merge_call(att.reshape(m_lat, ATTN_WIDTH), orec.reshape(m_lat, REC_WIDTH), z, x2d, mod3,
                         rec_gnorm_w[0][None, :], norm2_w[0][None, :],
                         w_ba16, w_br16, w_out16)
    g, w_down16 = _ffn_up_call(h2.reshape(BATCH, SEQ, D_MODEL), w_up16, conv_w[0], conv_b[0][None, :], w_down[0])
    out = _ffn_down_call(g.reshape(m_lat, FFN_DIM), w_down16, x1, mod3, final_norm_w[None, :])
    return out.reshape(BATCH, SEQ, D_MODEL)
```

```python
import math

import jax
import jax.numpy as jnp
from jax import lax
from jax.experimental import pallas as pl
from jax.experimental.pallas import tpu as pltpu

F32 = jnp.float32
BF16 = jnp.bfloat16

D_MODEL = 2048
BATCH = 8
SEQ = 2048
GRID_W = 64
CTX_LEN = 256
EPS = 1e-6
N_MOD = 6
HEADS = 8
HEAD_DIM = 64
HEAD_W = 128
ATTN_WIDTH = 1024
REC_WIDTH = 1024
FFN_DIM = 5632
IN_WIDTH = 12288
CTX_KV_WIDTH = 5120
ROPE_BASE = 10000.0
LAM_INIT = 0.8 - 0.6 * math.exp(-0.3 * 0)

COL_AK, COL_AV, COL_RFF, COL_RFB, COL_RI, COL_AQ, COL_RQ = 0, 8, 16, 24, 32, 40, 48
MOD_ROWS = 16
CTX_MOD_ROW = BATCH

CHUNK = 64
SUB = 16
LOG2E = 1.4426950408889634

VMEM_LIMIT = 56 * 1024 * 1024
MOD_TN = 1024
INPROJ_TM = 1024
INPROJ_TN, INPROJ_CTX_TN = 1536, 1280
ATTN_SUB = 512
MERGE_TM = 256
FFN_TN = 512
FFN_COLS = 256
FFN_DOWN_TM = 512


def _cparams(n_axes, vmem=None):
    return pltpu.CompilerParams(
        dimension_semantics=("arbitrary",) * n_axes,
        vmem_limit_bytes=vmem,
    )


def _sigmoid(x):
    return 1.0 / (1.0 + jnp.exp2(x * (-LOG2E)))


def _rms(x, w):
    return x * lax.rsqrt(jnp.mean(x * x, axis=-1, keepdims=True) + EPS) * w


def _mod_kernel(c_ref, w_ref, b_ref, o_ref):
    c = c_ref[...]
    a = (c * _sigmoid(c)).astype(BF16)
    o_ref[...] = jnp.dot(a, w_ref[...].astype(BF16), preferred_element_type=F32) + b_ref[...]


def _mod_call(c_all, w_mod, b_mod):
    tn = MOD_TN
    n = w_mod.shape[1]
    return pl.pallas_call(
        _mod_kernel,
        grid=(n // tn,),
        in_specs=[
            pl.BlockSpec((MOD_ROWS, D_MODEL), lambda j: (0, 0)),
            pl.BlockSpec((D_MODEL, tn), lambda j: (0, j)),
            pl.BlockSpec((1, tn), lambda j: (0, j)),
        ],
        out_specs=pl.BlockSpec((MOD_ROWS, tn), lambda j: (0, j)),
        out_shape=jax.ShapeDtypeStruct((MOD_ROWS, n), F32),
        compiler_params=_cparams(1, VMEM_LIMIT),
        name="mod",
    )(c_all, w_mod, b_mod)


def _inproj_kernel(x_ref, nw_ref, sh_ref, sc_ref, w_ref, o_ref, h_ref):
    @pl.when(pl.program_id(1) == 0)
    def _():
        h = _rms(x_ref[...], nw_ref[...]) * (1.0 + sc_ref[0]) + sh_ref[0]
        h_ref[...] = h.astype(BF16)

    o_ref[...] = jnp.dot(h_ref[...], w_ref[...], preferred_element_type=F32)


def _inproj_call(x2d, norm_w, mod3, w_bf16, n_out, mod_row_of_tile, tm, tn, name):
    m = x2d.shape[0]
    return pl.pallas_call(
        _inproj_kernel,
        grid=(m // tm, n_out // tn),
        in_specs=[
            pl.BlockSpec((tm, D_MODEL), lambda i, j: (i, 0)),
            pl.BlockSpec((1, D_MODEL), lambda i, j: (0, 0)),
            pl.BlockSpec((1, 1, D_MODEL), lambda i, j: (mod_row_of_tile(i), 0, 0)),
            pl.BlockSpec((1, 1, D_MODEL), lambda i, j: (mod_row_of_tile(i), 0, 1)),
            pl.BlockSpec((D_MODEL, tn), lambda i, j: (0, j)),
        ],
        out_specs=pl.BlockSpec((tm, tn), lambda i, j: (i, j)),
        out_shape=jax.ShapeDtypeStruct((m, n_out), F32),
        scratch_shapes=[pltpu.VMEM((tm, D_MODEL), BF16)],
        compiler_params=_cparams(2, VMEM_LIMIT),
        name=name,
    )(x2d, norm_w, mod3, mod3, w_bf16)


def _rope(x, c, sa, sb):
    return x * c + pltpu.roll(x, HEAD_W - 16, 1) * sa + pltpu.roll(x, 16, 1) * sb


def _attn_kernel(lamp_ref, q_ref, k_ref, v_ref, kc_ref, vc_ref, c_ref, sa_ref, sb_ref, subln_ref, o_ref, k_s, v_s):
    n_sub = SEQ // ATTN_SUB
    c, sa, sb = c_ref[...], sa_ref[...], sb_ref[...]

    k_s[0:SEQ, :] = _rope(k_ref[0], c, sa, sb).astype(BF16)
    k_s[SEQ:SEQ + CTX_LEN, :] = kc_ref[0].astype(BF16)
    v_s[0:SEQ, 0:HEAD_W] = v_ref[0].astype(BF16)
    v_s[SEQ:SEQ + CTX_LEN, 0:HEAD_W] = vc_ref[0].astype(BF16)
    v_s[:, HEAD_W:2 * HEAD_W] = jnp.ones((SEQ + CTX_LEN, HEAD_W), BF16)

    lp = lamp_ref[...]
    lam = (jnp.exp(jnp.sum(lp[0:1] * lp[1:2], axis=-1, keepdims=True))
           - jnp.exp(jnp.sum(lp[2:3] * lp[3:4], axis=-1, keepdims=True)) + LAM_INIT)

    q = _rope(q_ref[0], c, sa, sb) * (HEAD_DIM ** -0.5 * LOG2E)
    lane = lax.broadcasted_iota(jnp.int32, q.shape, 1)
    q0 = jnp.where(lane < HEAD_DIM, q, 0.0).astype(BF16)
    q1 = jnp.where(lane >= HEAD_DIM, q, 0.0).astype(BF16)

    def scores(i):
        lo, hi = i * ATTN_SUB, (i + 1) * ATTN_SUB
        qq = jnp.concatenate([q0[lo:hi], q1[lo:hi]], axis=0)
        return lax.dot_general(qq, k_s[...], (((1,), (1,)), ((), ())), preferred_element_type=F32)

    s_next = scores(0)
    for i in range(n_sub):
        s = s_next
        if i + 1 < n_sub:
            s_next = scores(i + 1)
        e = jnp.exp2(s - jnp.max(s, axis=-1, keepdims=True)).astype(BF16)
        r = jnp.dot(e, v_s[...], preferred_element_type=F32)
        p = r[:, 0:HEAD_W] / r[:, HEAD_W:2 * HEAD_W]
        o = p[0:ATTN_SUB] - lam * p[ATTN_SUB:2 * ATTN_SUB]
        o = _rms(o, subln_ref[...]) * (1.0 - LAM_INIT)
        o_ref[0, i * ATTN_SUB:(i + 1) * ATTN_SUB, :] = o.astype(BF16)


def _attn_call(z3, zc3, lam_params, rope_tabs, subln_w):
    lat = lambda col: pl.BlockSpec((1, SEQ, HEAD_W), lambda b, h: (b, 0, col + h))
    ctx = lambda col: pl.BlockSpec((1, CTX_LEN, HEAD_W), lambda b, h: (b, 0, col + h))
    full = lambda shape: pl.BlockSpec(shape, lambda b, h: (0, 0))
    return pl.pallas_call(
        _attn_kernel,
        grid=(BATCH, HEADS),
        in_specs=[full((4, HEAD_DIM)), lat(COL_AQ), lat(COL_AK), lat(COL_AV), ctx(COL_AK), ctx(COL_AV),
                  full((SEQ, HEAD_W)), full((SEQ, HEAD_W)), full((SEQ, HEAD_W)), full((1, HEAD_W))],
        out_specs=pl.BlockSpec((1, SEQ, HEAD_W), lambda b, h: (b, 0, h)),
        out_shape=jax.ShapeDtypeStruct((BATCH, SEQ, ATTN_WIDTH), BF16),
        scratch_shapes=[pltpu.VMEM((SEQ + CTX_LEN, HEAD_W), BF16),
                        pltpu.VMEM((SEQ + CTX_LEN, 2 * HEAD_W), BF16)],
        compiler_params=_cparams(2, VMEM_LIMIT),
        name="attn",
    )(lam_params, z3, z3, z3, zc3, zc3, *rope_tabs, subln_w)


LAT_CHUNKS = SEQ // CHUNK
CTX_CHUNKS = CTX_LEN // CHUNK
PREP_ROWS = 1024
INTRA_GROUP = 2
BOUNDED_GROUP = 32
DIAG_LOG2_BOUND = 96.0
SCAN_UNROLL = 32


def _tri(rev):
    r = lax.broadcasted_iota(jnp.int32, (CHUNK, CHUNK), 0)
    c = lax.broadcasted_iota(jnp.int32, (CHUNK, CHUNK), 1)
    return jnp.where((c >= r) if rev else (c <= r), 1.0, 0.0).astype(F32)


def _gate(f_raw, lower):
    f = lower + (1.0 - lower) * _sigmoid(f_raw)
    return 1.0 - f, jnp.log2(f)


def _prep(f_raw, lower, rev):
    n = f_raw.shape[0] // CHUNK
    k, logf = _gate(f_raw, lower)
    wide = jnp.concatenate([logf[i * CHUNK:(i + 1) * CHUNK] for i in range(n)], axis=1)
    tri = _tri(rev).astype(BF16)
    hi = wide.astype(BF16)
    rest = wide - hi.astype(F32)
    mid = rest.astype(BF16)
    lo = (rest - mid.astype(F32)).astype(BF16)
    bw = (jnp.dot(tri, hi, preferred_element_type=F32) + jnp.dot(tri, mid, preferred_element_type=F32)
          + jnp.dot(tri, lo, preferred_element_type=F32))
    t = 0 if rev else CHUNK - 1
    b = jnp.concatenate([bw[:, i * HEAD_W:(i + 1) * HEAD_W] for i in range(n)], axis=0)
    tots = [bw[t:t + 1, i * HEAD_W:(i + 1) * HEAD_W] for i in range(n)]
    b_tot = jnp.concatenate([jnp.broadcast_to(r, (CHUNK, HEAD_W)) for r in tots], axis=0)
    return k, b, b_tot, tots


def _diag_block(q_s, k_d, b_d, v_ref, base, rev):
    rows = lax.broadcasted_iota(jnp.int32, (8, HEAD_W), 0)
    halves = range(SUB // 8)
    qh = [q_s[pl.ds(base + 8 * u, 8), :] for u in halves]
    bh = [b_d[pl.ds(base + 8 * u, 8), :] for u in halves]
    acc = [jnp.zeros((8, HEAD_W), F32) for _ in halves]
    for s in range(SUB):
        bs = b_d[pl.ds(base + s, 1), :]
        ks = k_d[pl.ds(base + s, 1), :]
        vs = v_ref[0, pl.ds(base + s, 1), :]
        for u in halves:
            t_lo, t_hi = 8 * u, 8 * u + 7
            if (t_lo > s) if rev else (t_hi < s):
                continue
            w = jnp.exp2(bh[u] - bs)
            if (t_hi > s) if rev else (t_lo < s):
                keep = (rows + t_lo <= s) if rev else (rows + t_lo >= s)
                w = jnp.where(keep, w, 0.0)
            col = jnp.sum(qh[u] * w * ks, axis=-1, keepdims=True)
            acc[u] = acc[u] + col * vs
    return jnp.concatenate(acc, axis=0)


def _block_scores(q, k, b, rev, with_diag):
    blocks = []
    for i in range(CHUNK // SUB):
        lo, hi = SUB * i, SUB * (i + 1)
        if rev:
            e_lo, e_hi = (lo if with_diag else hi), CHUNK
            ref_row = b[hi:hi + 1] if hi < CHUNK else None
        else:
            e_lo, e_hi = 0, (hi if with_diag else lo)
            ref_row = b[lo - 1:lo] if lo > 0 else None
        if ref_row is None:
            if not with_diag:
                blocks.append(jnp.zeros((SUB, CHUNK), F32))
                continue
            ref_row = jnp.zeros((1, HEAD_W), F32)
        qt = (q[lo:hi] * jnp.exp2(b[lo:hi] - ref_row)).astype(BF16)
        kt = (k[e_lo:e_hi] * jnp.exp2(ref_row - b[e_lo:e_hi])).astype(BF16)
        parts = []
        if e_lo > 0:
            parts.append(jnp.zeros((e_lo, HEAD_W), BF16))
        parts.append(kt)
        if e_hi < CHUNK:
            parts.append(jnp.zeros((CHUNK - e_hi, HEAD_W), BF16))
        ktp = jnp.concatenate(parts, axis=0)
        blocks.append(lax.dot_general(qt, ktp, (((1,), (1,)), ((), ())), preferred_element_type=F32))
    return jnp.concatenate(blocks, axis=0)


N_RIDERS = 4


def _hgrn2_kernel(lb_ref, ff_ref, fb_ref, v_ref, q_ref, ffc_ref, fbc_ref, vc_ref, *refs):
    cast_in, (o_ref, *cast_out) = refs[:N_RIDERS], refs[N_RIDERS:2 * N_RIDERS + 1]
    q_s, k_s, b_s, qbar_s, kbar_s, dec_s, kbarc_s, decc_s, ob_s = refs[2 * N_RIDERS + 1:]
    for w_f32, w_bf16 in zip(cast_in, cast_out):
        w_bf16[...] = w_f32[...].astype(BF16)

    lb = lb_ref[...]
    lower = []
    for d in range(2):
        l0, l1 = lb[d, 0:1], lb[d, 1:2]
        mx = jnp.maximum(l0, l1)
        e0, e1 = jnp.exp(l0 - mx), jnp.exp(l1 - mx)
        lower.append(e0 / (e0 + e1))
    f_lat = (ff_ref, fb_ref)
    f_ctx = (ffc_ref, fbc_ref)

    for d in range(2):
        k, b, b_tot, tots = _prep(f_ctx[d][0], lower[d], d == 1)
        kbarc_s[d] = (k * jnp.exp2(b_tot - b)).astype(BF16)
        for c in range(CTX_CHUNKS):
            decc_s[d, c] = jnp.broadcast_to(jnp.exp2(tots[c]), (8, HEAD_W))

    def prep_body(g, carry):
        rows = pl.ds(pl.multiple_of(g * PREP_ROWS, PREP_ROWS), PREP_ROWS)
        qr = q_ref[0, rows, :]
        q = qr * _sigmoid(qr)
        q_s[rows, :] = q
        for d in range(2):
            k, b, b_tot, tots = _prep(f_lat[d][0, rows, :], lower[d], d == 1)
            k_s[d, rows, :] = k
            b_s[d, rows, :] = b
            qbar_s[d, rows, :] = (q * jnp.exp2(b)).astype(BF16)
            kbar_s[d, rows, :] = (k * jnp.exp2(b_tot - b)).astype(BF16)
            for i in range(PREP_ROWS // CHUNK):
                dec_s[d, g * (PREP_ROWS // CHUNK) + i] = jnp.broadcast_to(jnp.exp2(tots[i]), (8, HEAD_W))
        return carry

    lax.fori_loop(0, SEQ // PREP_ROWS, prep_body, 0)

    def chunk_scores(base, with_diag):
        rows = pl.ds(base, CHUNK)
        out = []
        for d in range(2):
            a = _block_scores(q_s[rows, :], k_s[d, rows, :], b_s[d, rows, :], d == 1, with_diag)
            if with_diag:
                r = lax.broadcasted_iota(jnp.int32, a.shape, 0)
                c = lax.broadcasted_iota(jnp.int32, a.shape, 1)
                a = jnp.where((c >= r) if d == 1 else (c <= r), a, 0.0)
            out.append(a.astype(BF16))
        return out

    def value_products(bases, scores):
        pv = []
        for j, base in enumerate(bases):
            v16 = v_ref[0, pl.ds(base, CHUNK), :].astype(BF16)
            pv.append(jnp.dot(scores[j][0], v16, preferred_element_type=F32)
                      + jnp.dot(scores[j][1], v16, preferred_element_type=F32))
        return pv

    def intra_body(g, carry):
        bases = [pl.multiple_of((g * INTRA_GROUP + j) * CHUNK, CHUNK) for j in range(INTRA_GROUP)]

        def diag(base):
            blocks = []
            for i in range(CHUNK // SUB):
                blk = pl.multiple_of(base + SUB * i, SUB)
                blocks.append(_diag_block(q_s, k_s.at[0], b_s.at[0], v_ref, blk, False)
                              + _diag_block(q_s, k_s.at[1], b_s.at[1], v_ref, blk, True))
            return jnp.concatenate(blocks, axis=0)

        scores = [chunk_scores(base, False) for base in bases]
        od = [diag(bases[0])]
        pv = value_products(bases, scores)
        od += [diag(base) for base in bases[1:]]
        for j, base in enumerate(bases):
            rows = pl.ds(base, CHUNK)
            o_ref[0, rows, :] = (o_ref[0, rows, :] + ob_s[rows, :]) + (pv[j] + od[j])
        return carry

    def intra_body_bounded(g, carry):
        bases = [pl.multiple_of((g * BOUNDED_GROUP + j) * CHUNK, CHUNK) for j in range(BOUNDED_GROUP)]
        pv = value_products(bases, [chunk_scores(base, True) for base in bases])
        for j, base in enumerate(bases):
            rows = pl.ds(base, CHUNK)
            o_ref[0, rows, :] = (o_ref[0, rows, :] + ob_s[rows, :]) + pv[j]
        return carry

    def ctx_states():
        steps = [(d, (CTX_CHUNKS - 1 - c) if d == 1 else c) for c in range(CTX_CHUNKS) for d in range(2)]
        upd = [lax.dot_general(vc_ref[0, cc * CHUNK:(cc + 1) * CHUNK, :].astype(BF16),
                               kbarc_s[d, cc * CHUNK:(cc + 1) * CHUNK, :], (((0,), (0,)), ((), ())),
                               preferred_element_type=F32) for d, cc in steps]
        sts = [None, None]
        for n, (d, cc) in enumerate(steps):
            sts[d] = upd[n] if sts[d] is None else decc_s[d, cc][0:1] * sts[d] + upd[n]
        return tuple(sts)

    def lat_body(it, sts):
        sts = list(sts)
        steps = []
        for u in range(SCAN_UNROLL):
            c = it * SCAN_UNROLL + u
            for d in range(2):
                cc = (LAT_CHUNKS - 1 - c) if d == 1 else c
                steps.append((d, cc, pl.ds(pl.multiple_of(cc * CHUNK, CHUNK), CHUNK)))
        upd = [lax.dot_general(v_ref[0, rows, :].astype(BF16), kbar_s[d, rows, :], (((0,), (0,)), ((), ())),
                               preferred_element_type=F32) for d, cc, rows in steps]
        for n, (d, cc, rows) in enumerate(steps):
            inter = lax.dot_general(qbar_s[d, rows, :], sts[d].astype(BF16), (((1,), (1,)), ((), ())),
                                    preferred_element_type=F32)
            if d == 0:
                o_ref[0, rows, :] = inter
            else:
                ob_s[rows, :] = inter
            sts[d] = dec_s[d, cc][0:1] * sts[d] + upd[n]
        return tuple(sts)

    lax.fori_loop(0, LAT_CHUNKS // SCAN_UNROLL, lat_body, ctx_states())
    worst = jnp.min(jnp.minimum(lower[0], lower[1]))
    bounded = worst >= 2.0 ** (-DIAG_LOG2_BOUND / SUB)

    @pl.when(bounded)
    def _():
        lax.fori_loop(0, LAT_CHUNKS // BOUNDED_GROUP, intra_body_bounded, 0)

    @pl.when(jnp.logical_not(bounded))
    def _():
        lax.fori_loop(0, LAT_CHUNKS // INTRA_GROUP, intra_body, 0)


def _hgrn2_call(z3, zc3, rec_lb, riders):
    lat = lambda col: pl.BlockSpec((1, SEQ, HEAD_W), lambda b, h: (b, 0, col + h))
    ctx = lambda col: pl.BlockSpec((1, CTX_LEN, HEAD_W), lambda b, h: (b, 0, col + h))
    assert len(riders) == N_RIDERS
    slab = lambda w: pl.BlockSpec((w.shape[0] // (BATCH * HEADS), w.shape[1]), lambda b, h: (b * HEADS + h, 0))
    return pl.pallas_call(
        _hgrn2_kernel,
        grid=(BATCH, HEADS),
        in_specs=[
            pl.BlockSpec((2, 2, HEAD_W), lambda b, h: (0, 0, h)),
            lat(COL_RFF), lat(COL_RFB), lat(COL_RI), lat(COL_RQ),
            ctx(COL_RFF), ctx(COL_RFB), ctx(COL_RI),
        ] + [slab(w) for w in riders],
        out_specs=[pl.BlockSpec((1, SEQ, HEAD_W), lambda b, h: (b, 0, h))] + [slab(w) for w in riders],
        out_shape=[jax.ShapeDtypeStruct((BATCH, SEQ, REC_WIDTH), F32)]
        + [jax.ShapeDtypeStruct(w.shape, BF16) for w in riders],
        scratch_shapes=[
            pltpu.VMEM((SEQ, HEAD_W), F32),
            pltpu.VMEM((2, SEQ, HEAD_W), F32),
            pltpu.VMEM((2, SEQ, HEAD_W), F32),
            pltpu.VMEM((2, SEQ, HEAD_W), BF16),
            pltpu.VMEM((2, SEQ, HEAD_W), BF16),
            pltpu.VMEM((2, LAT_CHUNKS, 8, HEAD_W), F32),
            pltpu.VMEM((2, CTX_LEN, HEAD_W), BF16),
            pltpu.VMEM((2, CTX_CHUNKS, 8, HEAD_W), F32),
            pltpu.VMEM((SEQ, HEAD_W), F32),
        ],
        compiler_params=_cparams(2, VMEM_LIMIT),
        name="hgrn2",
    )(rec_lb, z3, z3, z3, z3, zc3, zc3, zc3, *riders)


def _merge_kernel(att_ref, orec_ref, rg_ref, ga_ref, gr_ref, x_ref, g1_ref, sh2_ref, sc2_ref,
                  gnw_ref, n2w_ref, wba_ref, wbr_ref, wout_ref, x1_ref, h2_ref):
    rg = rg_ref[...]
    rec = _rms(orec_ref[...], gnw_ref[...]) * (rg * _sigmoid(rg))
    ya = jnp.dot(att_ref[...], wba_ref[...], preferred_element_type=F32)
    yr = jnp.dot(rec.astype(BF16), wbr_ref[...], preferred_element_type=F32)
    y = _sigmoid(ga_ref[...]) * ya + _sigmoid(gr_ref[...]) * yr
    x1 = x_ref[...] + g1_ref[0] * jnp.dot(y.astype(BF16), wout_ref[...], preferred_element_type=F32)
    x1_ref[...] = x1
    h2 = _rms(x1, n2w_ref[...]) * (1.0 + sc2_ref[0]) + sh2_ref[0]
    h2_ref[...] = h2.astype(BF16)


def _merge_call(att2d, orec2d, z2d, x2d, mod3, gnorm_w, norm2_w, wba, wbr, wout):
    tm = MERGE_TM
    m = x2d.shape[0]
    row = lambda i: (i, 0)
    const = lambda i: (0, 0)
    modspec = lambda k: pl.BlockSpec((1, 1, D_MODEL), lambda i: ((i * tm) // SEQ, 0, k))
    resident = lambda shape: pl.BlockSpec(shape, const, pipeline_mode=pl.Buffered(1))
    return pl.pallas_call(
        _merge_kernel,
        grid=(m // tm,),
        in_specs=[
            pl.BlockSpec((tm, ATTN_WIDTH), row),
            pl.BlockSpec((tm, REC_WIDTH), row),
            pl.BlockSpec((tm, REC_WIDTH), lambda i: (i, 7)),
            pl.BlockSpec((tm, D_MODEL), lambda i: (i, 4)),
            pl.BlockSpec((tm, D_MODEL), lambda i: (i, 5)),
            pl.BlockSpec((tm, D_MODEL), row),
            modspec(2), modspec(3), modspec(4),
            pl.BlockSpec((1, REC_WIDTH), const),
            pl.BlockSpec((1, D_MODEL), const),
            resident((ATTN_WIDTH, D_MODEL)),
            resident((REC_WIDTH, D_MODEL)),
            resident((D_MODEL, D_MODEL)),
        ],
        out_specs=[pl.BlockSpec((tm, D_MODEL), row), pl.BlockSpec((tm, D_MODEL), row)],
        out_shape=[jax.ShapeDtypeStruct((m, D_MODEL), F32), jax.ShapeDtypeStruct((m, D_MODEL), BF16)],
        compiler_params=_cparams(1, VMEM_LIMIT),
        name="merge",
    )(att2d, orec2d, z2d, z2d, z2d, x2d, mod3, mod3, mod3, gnorm_w, norm2_w, wba, wbr, wout)


FFN_PAD = 8


def _ffn_up_kernel(h_ref, wa_ref, wb_ref, cwa_ref, cwb_ref, cba_ref, cbb_ref, wd_ref, g_ref, wd16_ref, u_s):
    wd16_ref[...] = wd_ref[...].astype(BF16)
    h = h_ref[0]
    n_units = wa_ref.shape[1] // FFN_COLS
    units = []
    for c in range(n_units):
        cols = slice(c * FFN_COLS, (c + 1) * FFN_COLS)
        units.append((wa_ref, cwa_ref, cba_ref, cols))
        units.append((wb_ref, cwb_ref, cbb_ref, cols))

    pad = jnp.zeros((FFN_PAD, FFN_COLS), F32)
    for slot in range(2):
        u_s[slot, 0:FFN_PAD, :] = pad
        u_s[slot, FFN_PAD + SEQ:2 * FFN_PAD + SEQ, :] = pad

    def proj(n):
        w_ref, _, _, cols = units[n]
        u_s[n % 2, FFN_PAD:FFN_PAD + SEQ, :] = jnp.dot(h, w_ref[:, cols], preferred_element_type=F32)

    def conv(n):
        _, cw_ref, cb_ref, cols = units[n]
        cw = cw_ref[:, cols]
        buf = u_s.at[n % 2]
        return cb_ref[:, cols] + (buf[FFN_PAD - 1:FFN_PAD - 1 + SEQ, :] * cw[0:1]
                                  + buf[FFN_PAD:FFN_PAD + SEQ, :] * cw[1:2]
                                  + buf[FFN_PAD + 1:FFN_PAD + 1 + SEQ, :] * cw[2:3])

    proj(0)
    a = None
    for n in range(len(units)):
        if n + 1 < len(units):
            proj(n + 1)
        y = conv(n)
        if n % 2 == 0:
            a = y * _sigmoid(y)
        else:
            g_ref[0, :, units[n][3]] = (a * y).astype(BF16)


def _ffn_up_call(h3, w_up, conv_w, conv_b, w_down):
    tn = FFN_TN
    nb = FFN_DIM // tn
    wd_slab = pl.BlockSpec((FFN_DIM // (BATCH * nb), D_MODEL), lambda b, j: (b * nb + j, 0))
    return pl.pallas_call(
        _ffn_up_kernel,
        grid=(BATCH, nb),
        in_specs=[
            pl.BlockSpec((1, SEQ, D_MODEL), lambda b, j: (b, 0, 0)),
            pl.BlockSpec((D_MODEL, tn), lambda b, j: (0, j)),
            pl.BlockSpec((D_MODEL, tn), lambda b, j: (0, nb + j)),
            pl.BlockSpec((3, tn), lambda b, j: (0, j)),
            pl.BlockSpec((3, tn), lambda b, j: (0, nb + j)),
            pl.BlockSpec((1, tn), lambda b, j: (0, j)),
            pl.BlockSpec((1, tn), lambda b, j: (0, nb + j)),
            wd_slab,
        ],
        out_specs=[pl.BlockSpec((1, SEQ, tn), lambda b, j: (b, 0, j)), wd_slab],
        out_shape=[jax.ShapeDtypeStruct((BATCH, SEQ, FFN_DIM), BF16),
                   jax.ShapeDtypeStruct(w_down.shape, BF16)],
        scratch_shapes=[pltpu.VMEM((2, SEQ + 2 * FFN_PAD, FFN_COLS), F32)],
        compiler_params=_cparams(2, VMEM_LIMIT),
        name="ffn_up",
    )(h3, w_up, w_up, conv_w, conv_w, conv_b, conv_b, w_down)


def _ffn_down_kernel(g_ref, w_ref, x1_ref, g2_ref, fw_ref, o_ref):
    y = jnp.dot(g_ref[...], w_ref[...], preferred_element_type=F32)
    x2 = x1_ref[...] + g2_ref[0] * y
    o_ref[...] = _rms(x2, fw_ref[...])


def _ffn_down_call(g2d, w_down, x1, mod3, final_w):
    tm = FFN_DOWN_TM
    m = g2d.shape[0]
    return pl.pallas_call(
        _ffn_down_kernel,
        grid=(m // tm,),
        in_specs=[
            pl.BlockSpec((tm, FFN_DIM), lambda i: (i, 0)),
            pl.BlockSpec((FFN_DIM, D_MODEL), lambda i: (0, 0), pipeline_mode=pl.Buffered(1)),
            pl.BlockSpec((tm, D_MODEL), lambda i: (i, 0)),
            pl.BlockSpec((1, 1, D_MODEL), lambda i: ((i * tm) // SEQ, 0, 5)),
            pl.BlockSpec((1, D_MODEL), lambda i: (0, 0)),
        ],
        out_specs=pl.BlockSpec((tm, D_MODEL), lambda i: (i, 0)),
        out_shape=jax.ShapeDtypeStruct((m, D_MODEL), F32),
        compiler_params=_cparams(1, VMEM_LIMIT),
        name="ffn_down",
    )(g2d, w_down, x1, mod3, final_w)


def _rope_tables():
    rows = SEQ // GRID_W
    r, col = jnp.meshgrid(jnp.arange(rows), jnp.arange(GRID_W), indexing="ij")
    pos = jnp.stack([r.reshape(-1), col.reshape(-1)], axis=-1).astype(F32)
    nq = HEAD_DIM // 4
    inv = ROPE_BASE ** (-jnp.arange(nq, dtype=F32) / nq)
    ang = pos[:, :, None] * inv
    cos, sin = jnp.cos(ang), jnp.sin(ang)
    lane = jnp.arange(HEAD_W)
    axis = (lane % HEAD_DIM) // (2 * nq)
    second = ((lane % (2 * nq)) // nq) == 1
    freq = lane % nq
    c_t = cos[:, axis, freq]
    s_t = sin[:, axis, freq]
    sa_t = jnp.where(second[None, :], 0.0, -s_t)
    sb_t = jnp.where(second[None, :], s_t, 0.0)
    return c_t, sa_t, sb_t


def kernel(x, c, ctx, c_ctx, w_mod, b_mod, norm1_w, w_in, lam_q1, lam_k1, lam_q2, lam_k2, subln_w,
           rec_lb, rec_gnorm_w, w_branch_attn, w_branch_rec, w_out, norm2_w, w_up, conv_w, conv_b,
           w_down, final_norm_w):
    m_lat = BATCH * SEQ
    c_all = jnp.concatenate([c, c_ctx[None, :], jnp.zeros((MOD_ROWS - BATCH - 1, D_MODEL), F32)], axis=0)
    mod = _mod_call(c_all, w_mod[0], b_mod[0][None, :])
    mod3 = mod.reshape(MOD_ROWS, 1, N_MOD * D_MODEL)

    w_in16 = w_in[0].astype(BF16)
    n1w = norm1_w[0][None, :]
    x2d = x.reshape(m_lat, D_MODEL)
    z = _inproj_call(x2d, n1w, mod3, w_in16, IN_WIDTH, lambda i: (i * INPROJ_TM) // SEQ, INPROJ_TM, INPROJ_TN,
                     "inproj")
    zc = _inproj_call(ctx.reshape(BATCH * CTX_LEN, D_MODEL), n1w, mod3, w_in16, CTX_KV_WIDTH,
                      lambda i: CTX_MOD_ROW, INPROJ_TM, INPROJ_CTX_TN, "inproj_ctx")
    z3 = z.reshape(BATCH, SEQ, IN_WIDTH)
    zc3 = zc.reshape(BATCH, CTX_LEN, CTX_KV_WIDTH)

    lam_params = jnp.stack([lam_q1[0], lam_k1[0], lam_q2[0], lam_k2[0]], axis=0)
    att = _attn_call(z3, zc3, lam_params, _rope_tables(), subln_w[0][None, :])
    orec, w_up16, w_out16, w_ba16, w_br16 = _hgrn2_call(
        z3, zc3, rec_lb, [w_up[0], w_out[0], w_branch_attn[0], w_branch_rec[0]])

    x1, h2 = _merge_call(att.reshape(m_lat, ATTN_WIDTH), orec.reshape(m_lat, REC_WIDTH), z, x2d, mod3,
                         rec_gnorm_w[0][None, :], norm2_w[0][None, :],
                         w_ba16, w_br16, w_out16)
    g, w_down16 = _ffn_up_call(h2.reshape(BATCH, SEQ, D_MODEL), w_up16, conv_w[0], conv_b[0][None, :], w_down[0])
    out = _ffn_down_call(g.reshape(m_lat, FFN_DIM), w_down16, x1, mod3, final_norm_w[None, :])
    return out.reshape(BATCH, SEQ, D_MODEL)
```
